```python
import jax, jax.numpy as jnp
from jax import lax
import numpy as np

D_MODEL = 1024
BATCH = 16
SEQ = 2048
DEPTH = 1

N_MEM = 256
HEAD_DIM = 64
RWKV_HEADS = 6
RWKV_WIDTH = RWKV_HEADS * HEAD_DIM
DECAY_LORA = 32
ICLR_LORA = 32
GATE_LORA = 64
RWKV_IN = 3 * RWKV_WIDTH + GATE_LORA + 2 * DECAY_LORA + 2 * ICLR_LORA
ATT_HEADS = 6
ATT_WIDTH = ATT_HEADS * HEAD_DIM
DILATED_BRANCHES = ((128, 1), (512, 4), (2048, 16))
MEM_HEADS = 4
MEM_WIDTH = MEM_HEADS * HEAD_DIM
MIX_WIDTH = RWKV_WIDTH + ATT_WIDTH + MEM_WIDTH
IN_WIDTH = RWKV_IN + 3 * ATT_WIDTH + MEM_WIDTH
N_EXPERTS = 32
TOP_K = 4
D_EXPERT = D_MODEL
SWIGLU_LIMIT = 7.0
SWIGLU_ALPHA = 1.702
MOE_BLOCK = 128
LN_EPS = 1e-5
RWKV_GN_EPS = 64e-5
NEG_INF = -1e30
DEEPNORM_ALPHA = (2 * DEPTH) ** 0.25
DEEPNORM_BETA = (8 * DEPTH) ** -0.25

kernel_name = "hybrid_rwkv7_dilated_alibi_memxattn_moe_deepnorm"


def layer_norm(x, g, b):
    xf = x.astype(jnp.float32)
    mu = jnp.mean(xf, -1, keepdims=True)
    var = jnp.mean(jnp.square(xf - mu), -1, keepdims=True)
    return ((xf - mu) * lax.rsqrt(var + LN_EPS) * g + b).astype(x.dtype)


def rwkv7_bidirectional(p, mu, w0, w_up, a0, a_up, g_up, k_k, k_a, r_k, gn_g, gn_b):
    B, S, _ = p.shape
    H, N, C = RWKV_HEADS, HEAD_DIM, RWKV_WIDTH
    f32 = jnp.float32
    prev = jnp.pad(p[:, :-1], ((0, 0), (1, 0), (0, 0)))
    nxt = jnp.pad(p[:, 1:], ((0, 0), (0, 1), (0, 0)))
    p = (p + mu * (0.5 * (prev + nxt) - p)).astype(f32)
    r, k, v = p[..., :C], p[..., C:2 * C], p[..., 2 * C:3 * C]
    o = 3 * C
    g_lo = p[..., o:o + GATE_LORA]
    o += GATE_LORA
    w_lo = p[..., o:o + 2 * DECAY_LORA].reshape(B, S, 2, DECAY_LORA)
    o += 2 * DECAY_LORA
    a_lo = p[..., o:].reshape(B, S, 2, ICLR_LORA)
    w_logit = w0 + jnp.einsum('bsdr,drc->bsdc', jnp.tanh(w_lo), w_up)
    decay = jnp.exp(-jnp.exp(-jax.nn.softplus(-w_logit) - 0.5))
    a = jax.nn.sigmoid(a0 + jnp.einsum('bsdr,drc->bsdc', a_lo, a_up))
    g = jax.nn.sigmoid(g_lo) @ g_up
    kk = (k * k_k).reshape(B, S, H, N)
    kk = kk / jnp.maximum(jnp.sqrt(jnp.sum(kk * kk, -1, keepdims=True)), 1e-12)
    k_dir = (k[:, :, None, :] * (1.0 + (a - 1.0) * k_a)).reshape(B, S, 2, H, N)
    r_h = r.reshape(B, S, H, N)
    v_h = v.reshape(B, S, H, N)

    def both(t):
        return jnp.stack([t, t], axis=2)

    def time_major(t):
        t = jnp.stack([t[:, :, 0], jnp.flip(t[:, :, 1], axis=1)], axis=0)
        return jnp.transpose(t, (2, 0, 1, 3, 4))

    kk2 = both(kk)
    xs = (time_major(both(r_h)), time_major(decay.reshape(B, S, 2, H, N)), time_major(k_dir),
          time_major(both(v_h)), time_major(kk2), time_major(kk2 * a.reshape(B, S, 2, H, N)))

    def step(state, inp):
        r_t, w_t, k_t, v_t, kk_t, b_t = inp
        sa = jnp.einsum('dbhij,dbhj->dbhi', state, kk_t)
        state = state * w_t[..., None, :] - sa[..., :, None] * b_t[..., None, :] + v_t[..., :, None] * k_t[..., None, :]
        return state, jnp.einsum('dbhij,dbhj->dbhi', state, r_t)

    _, ys = lax.scan(step, jnp.zeros((2, B, H, N, N), f32), xs)
    y = jnp.transpose(ys[:, 0] + jnp.flip(ys[:, 1], axis=0), (1, 0, 2, 3))
    mean = jnp.mean(y, -1, keepdims=True)
    var = jnp.mean(jnp.square(y - mean), -1, keepdims=True)
    y = ((y - mean) * lax.rsqrt(var + RWKV_GN_EPS)).reshape(B, S, C) * gn_g + gn_b
    bonus = jnp.sum(r_h[:, :, None] * k_dir * r_k, axis=(2, 4))[..., None] * v_h
    return (y + bonus.reshape(B, S, C)) * g


def dilated_branch(q, k, v, slopes, window, dil):
    B, H, S, N = q.shape
    half = window // (2 * dil)
    L = S // dil
    nb = -(-L // half)
    pad = nb * half - L

    def by_residue(t):
        return jnp.transpose(t.reshape(B, H, L, dil, N), (0, 1, 3, 2, 4))

    qb = jnp.pad(by_residue(q), ((0, 0), (0, 0), (0, 0), (0, pad), (0, 0))).reshape(B, H, dil, nb, half, N)

    def windows(t):
        t = jnp.pad(by_residue(t), ((0, 0), (0, 0), (0, 0), (half, pad + half), (0, 0))).reshape(B, H, dil, nb + 2, half, N)
        return jnp.concatenate([t[:, :, :, :-2], t[:, :, :, 1:-1], t[:, :, :, 2:]], axis=4)

    kw, vw = windows(k), windows(v)
    s = jnp.einsum('bhrnqd,bhrnkd->bhrnqk', qb, kw).astype(jnp.float32)
    qi = jnp.arange(half)[:, None]
    kj = jnp.arange(3 * half)[None, :]
    delta = kj - half - qi
    lk = jnp.arange(nb)[:, None, None] * half - half + kj[None]
    valid = (jnp.abs(delta) <= half)[None] & (lk >= 0) & (lk < L)
    dist = (jnp.abs(delta) * dil).astype(jnp.float32)
    s = jnp.where(valid, s - slopes[None, :, None, None, None, None] * dist, NEG_INF)
    m = jnp.max(s, -1, keepdims=True)
    pr = jnp.exp(s - m)
    den = jnp.sum(pr, -1, keepdims=True)
    o = jnp.einsum('bhrnqk,bhrnkd->bhrnqd', pr, vw) / den
    lse = (m + jnp.log(den))[..., 0]

    def back(t):
        t = t.reshape((B, H, dil, nb * half) + t.shape[5:])[:, :, :, :L]
        t = jnp.moveaxis(t, 2, 3)
        return t.reshape((B, H, S) + t.shape[4:])

    return back(o), back(lse)


def dilated_mixture_attention(q, k, v):
    slopes = jnp.exp2(-8.0 * jnp.arange(1, ATT_HEADS + 1, dtype=jnp.float32) / ATT_HEADS)
    outs, lses = [], []
    for window, dil in DILATED_BRANCHES:
        o_b, lse_b = dilated_branch(q, k, v, slopes, window, dil)
        outs.append(o_b)
        lses.append(lse_b)
    wts = jax.nn.softmax(jnp.stack(lses), axis=0)
    return jnp.sum(wts[..., None] * jnp.stack(outs), axis=0)


def parallel_mixer(h, mem, w_in, mu, w0, w_up, a0, a_up, g_up, k_k, k_a, r_k, gn_g, gn_b, w_mem_kv, w_out):
    B, S, _ = h.shape
    f32 = jnp.float32
    p = h @ w_in
    y_rwkv = rwkv7_bidirectional(p[..., :RWKV_IN], mu, w0, w_up, a0, a_up, g_up, k_k, k_a, r_k, gn_g, gn_b)
    o = RWKV_IN

    def heads_first(t, n):
        return jnp.transpose(t.reshape(B, S, n, HEAD_DIM), (0, 2, 1, 3)).astype(f32)

    q = heads_first(p[..., o:o + ATT_WIDTH], ATT_HEADS) * HEAD_DIM ** -0.5
    k = heads_first(p[..., o + ATT_WIDTH:o + 2 * ATT_WIDTH], ATT_HEADS)
    v = heads_first(p[..., o + 2 * ATT_WIDTH:o + 3 * ATT_WIDTH], ATT_HEADS)
    y_att = jnp.transpose(dilated_mixture_attention(q, k, v), (0, 2, 1, 3)).reshape(B, S, ATT_WIDTH)
    o += 3 * ATT_WIDTH
    q_m = p[..., o:].reshape(B, S, MEM_HEADS, HEAD_DIM).astype(f32)
    kv_m = (mem @ w_mem_kv).reshape(B, N_MEM, 2, MEM_HEADS, HEAD_DIM).astype(f32)
    s_m = jnp.einsum('bshd,bmhd->bhsm', q_m, kv_m[:, :, 0]) * HEAD_DIM ** -0.5
    y_mem = jnp.einsum('bhsm,bmhd->bshd', jax.nn.softmax(s_m, axis=-1), kv_m[:, :, 1]).reshape(B, S, MEM_WIDTH)
    y = jnp.concatenate([y_rwkv, y_att, y_mem], axis=-1).astype(h.dtype)
    return y @ w_out


def moe(h, w_router, b_router, w_gate_up, b_gate_up, w_down, b_down):
    B, S, D = h.shape
    T = B * S
    n_assign = T * TOP_K
    xt = h.reshape(T, D)
    logits = (xt @ w_router).astype(jnp.float32) + b_router
    top_v, top_e = lax.top_k(logits, TOP_K)
    gates = jax.nn.softmax(top_v, axis=-1)
    flat_e = top_e.reshape(-1)
    order = jnp.argsort(flat_e)
    sorted_e = flat_e[order]
    counts = jnp.bincount(flat_e, length=N_EXPERTS)
    padded = (counts + MOE_BLOCK - 1) // MOE_BLOCK * MOE_BLOCK
    starts = jnp.cumsum(counts) - counts
    pends = jnp.cumsum(padded)
    pstarts = pends - padded
    dest_sorted = pstarts[sorted_e] + jnp.arange(n_assign) - starts[sorted_e]
    dest = jnp.zeros((n_assign,), jnp.int32).at[order].set(dest_sorted.astype(jnp.int32))
    n_rows = n_assign + N_EXPERTS * MOE_BLOCK
    n_blocks = n_rows // MOE_BLOCK
    row_token = jnp.full((n_rows,), T, jnp.int32).at[dest].set(jnp.arange(n_assign, dtype=jnp.int32) // TOP_K)
    x_rows = jnp.concatenate([xt, jnp.zeros((1, D), xt.dtype)], axis=0)[row_token].reshape(n_blocks, MOE_BLOCK, D)
    block_e = jnp.minimum(jnp.searchsorted(pends, jnp.arange(n_blocks) * MOE_BLOCK, side='right'), N_EXPERTS - 1)

    def expert_block(args):
        xb, e = args
        gu = xb @ w_gate_up[e] + b_gate_up[e]
        gate = jnp.minimum(gu[:, :D_EXPERT], SWIGLU_LIMIT)
        up = jnp.clip(gu[:, D_EXPERT:], -SWIGLU_LIMIT, SWIGLU_LIMIT)
        act = (up + 1.0) * gate * jax.nn.sigmoid(SWIGLU_ALPHA * gate)
        return act @ w_down[e] + b_down[e]

    y_rows = lax.map(expert_block, (x_rows, block_e)).reshape(n_rows, D)
    y = jnp.einsum('tkd,tk->td', y_rows[dest].reshape(T, TOP_K, D), gates.astype(y_rows.dtype))
    return y.reshape(B, S, D)


def setup_inputs(seed: int = 0) -> dict:
    key = jax.random.key(seed)
    ks = jax.random.split(key, 32)
    f32 = jnp.float32
    L, C, D, E, F = DEPTH, RWKV_WIDTH, D_MODEL, N_EXPERTS, D_EXPERT

    def nrm(i, shape, scale):
        return scale * jax.random.normal(ks[i], shape, f32)

    ramp = (jnp.arange(C, dtype=f32) / (C - 1)) ** 0.9
    return {
        'x': nrm(0, (BATCH, SEQ, D), 1.0),
        'mem': nrm(1, (BATCH, N_MEM, D), 1.0),
        'ln_in_g': 1.0 + nrm(2, (D,), 0.02),
        'ln_in_b': nrm(3, (D,), 0.02),
        'w_in': nrm(4, (L, D, IN_WIDTH), D ** -0.5),
        'mu_shift': jax.random.uniform(ks[5], (L, RWKV_IN), f32),
        'w0': -5.5 + 5.0 * ramp + nrm(6, (L, 2, C), 0.1),
        'w_up': nrm(7, (L, 2, DECAY_LORA, C), 0.5 * DECAY_LORA ** -0.5),
        'a0': nrm(8, (L, 2, C), 0.1),
        'a_up': nrm(9, (L, 2, ICLR_LORA, C), 0.5 * ICLR_LORA ** -0.5),
        'g_up': nrm(10, (L, GATE_LORA, C), GATE_LORA ** -0.5),
        'k_k': 0.85 + nrm(11, (L, C), 0.05),
        'k_a': 1.0 + nrm(12, (L, C), 0.05),
        'r_k': nrm(13, (L, RWKV_HEADS, HEAD_DIM), 0.1),
        'gn_g': 1.0 + nrm(14, (L, C), 0.02),
        'gn_b': nrm(15, (L, C), 0.02),
        'w_mem_kv': nrm(16, (L, D, 2 * MEM_WIDTH), D ** -0.5),
        'w_out': nrm(17, (L, MIX_WIDTH, D), DEEPNORM_BETA * MIX_WIDTH ** -0.5),
        'ln1_g': 1.0 + nrm(18, (L, D), 0.02),
        'ln1_b': nrm(19, (L, D), 0.02),
        'w_router': nrm(20, (L, D, E), D ** -0.5),
        'b_router': nrm(21, (L, E), 0.01),
        'w_gate_up': nrm(22, (L, E, D, 2 * F), D ** -0.5),
        'b_gate_up': nrm(23, (L, E, 2 * F), 0.01),
        'w_down': nrm(24, (L, E, F, D), DEEPNORM_BETA * F ** -0.5),
        'b_down': nrm(25, (L, E, D), 0.01),
        'ln2_g': 1.0 + nrm(26, (L, D), 0.02),
        'ln2_b': nrm(27, (L, D), 0.02),
    }


def reference(x, mem, ln_in_g, ln_in_b, w_in, mu_shift, w0, w_up, a0, a_up, g_up, k_k, k_a, r_k, gn_g, gn_b,
              w_mem_kv, w_out, ln1_g, ln1_b, w_router, b_router, w_gate_up, b_gate_up, w_down, b_down, ln2_g, ln2_b):
    h = layer_norm(x, ln_in_g, ln_in_b)
    for l in range(DEPTH):
        mix = parallel_mixer(h, mem, w_in[l], mu_shift[l], w0[l], w_up[l], a0[l], a_up[l], g_up[l], k_k[l], k_a[l],
                             r_k[l], gn_g[l], gn_b[l], w_mem_kv[l], w_out[l])
        h = layer_norm(DEEPNORM_ALPHA * h + mix, ln1_g[l], ln1_b[l])
        ffn = moe(h, w_router[l], b_router[l], w_gate_up[l], b_gate_up[l], w_down[l], b_down[l])
        h = layer_norm(DEEPNORM_ALPHA * h + ffn, ln2_g[l], ln2_b[l])
    return h
```

```python
import functools
import math

import jax
import jax.numpy as jnp
from jax import lax
from jax.experimental import pallas as pl
from jax.experimental.pallas import tpu as pltpu

F32 = jnp.float32
BF16 = jnp.bfloat16

HEAD_DIM = 64
RWKV_HEADS = 6
RWKV_WIDTH = RWKV_HEADS * HEAD_DIM
GATE_LORA = 64
DECAY_LORA = 32
ICLR_LORA = 32
LANES = 128
RWKV_CHUNK = 64
RWKV_PAD_IN = 3 * RWKV_WIDTH + 2 * LANES
RWKV_GN_EPS = 64e-5
DECAY_SCALE = math.exp(-0.5)
VMEM_LIMIT = 56 * 1024 * 1024


def _bf(x):
    return x.astype(BF16)


def _dot(a, b):
    return jnp.dot(_bf(a), _bf(b), preferred_element_type=F32)


def _dot_nt(a, b):
    return lax.dot_general(_bf(a), _bf(b), (((1,), (1,)), ((), ())), preferred_element_type=F32)


def _dot_tn(a, b):
    return lax.dot_general(_bf(a), _bf(b), (((0,), (0,)), ((), ())), preferred_element_type=F32)


def _split3(x):
    hi = _bf(x)
    r1 = x - hi.astype(F32)
    mid = _bf(r1)
    lo = _bf(r1 - mid.astype(F32))
    return hi, mid, lo


def _dot_exact_lhs(a_bf, x, terms=3):
    parts = _split3(x)[:terms]
    acc = jnp.dot(a_bf, parts[0], preferred_element_type=F32)
    for p in parts[1:]:
        acc = acc + jnp.dot(a_bf, p, preferred_element_type=F32)
    return acc


def _dot_exact_rhs(x, b_bf, terms=2):
    parts = _split3(x)[:terms]
    acc = jnp.dot(parts[0], b_bf, preferred_element_type=F32)
    for p in parts[1:]:
        acc = acc + jnp.dot(p, b_bf, preferred_element_type=F32)
    return acc


def _sigmoid(x):
    return 1.0 / (1.0 + jnp.exp(-x))


def _rwkv_kernel(p_ref, mu_ref, wup_ref, aup_ref, gup_ref, w0_ref, a0_ref, kk_ref, ka_ref, rk_ref,
                 gng_ref, gnb_ref, hsum_ref, o_ref, state_ref, *, seq, chunk):
    C, N, H, W = chunk, HEAD_DIM, RWKV_HEADS, RWKV_WIDTH
    nc = seq // C
    c = pl.program_id(1)

    @pl.when(c == 0)
    def _():
        state_ref[...] = jnp.zeros_like(state_ref)

    hsum = hsum_ref[...]
    row = lax.broadcasted_iota(jnp.int32, (C, C), 0)
    col = lax.broadcasted_iota(jnp.int32, (C, C), 1)
    rowc = lax.broadcasted_iota(jnp.int32, (C, 1), 0)

    def shifted(ci):
        base = pl.multiple_of(ci * C, C)
        cur = p_ref[0, pl.ds(base, C), :]
        pb = p_ref[0, pl.ds(pl.multiple_of(jnp.maximum(base - 8, 0), 8), 8), :]
        nb = p_ref[0, pl.ds(pl.multiple_of(jnp.minimum(base + C, seq - 8), 8), 8), :]
        prev_row = jnp.where(ci > 0, pb[7:8, :], 0.0)
        next_row = jnp.where(ci < nc - 1, nb[0:1, :], 0.0)
        prev = jnp.where(rowc == 0, prev_row, pltpu.roll(cur, 1, 0))
        nxt = jnp.where(rowc == C - 1, next_row, pltpu.roll(cur, C - 1, 0))
        return cur + mu_ref[...] * (0.5 * (prev + nxt) - cur)

    def one_direction(d, ci):
        ps = shifted(ci)
        r = ps[:, 0:W]
        k = ps[:, W:2 * W]
        v = ps[:, 2 * W:3 * W]
        x1 = ps[:, 3 * W:3 * W + LANES]
        x2 = ps[:, 3 * W + LANES:3 * W + 2 * LANES]
        a_both = [_sigmoid(a0_ref[e:e + 1, :] + _dot(x2, aup_ref[e])) for e in range(2)]
        w_logit = w0_ref[d:d + 1, :] + _dot(jnp.tanh(x1), wup_ref[d])
        logw = -DECAY_SCALE * _sigmoid(w_logit)
        kkr = k * kk_ref[...]
        ss = _dot_exact_rhs(kkr * kkr, hsum)
        kk = kkr / jnp.maximum(jnp.sqrt(ss), 1e-12)
        kdir_both = [k * (1.0 + (a - 1.0) * ka_ref[...]) for a in a_both]
        kdir = kdir_both[d]
        b = kk * a_both[d]

        if d == 0:
            incl = row >= col
            strict = row > col
        else:
            incl = row <= col
            strict = row < col
        tri = jnp.where(incl, 1.0, 0.0).astype(BF16)
        lin = _dot_exact_lhs(tri, logw)
        tot = jnp.sum(logw, axis=0, keepdims=True)
        e_in = jnp.exp(lin)
        e_ex = jnp.exp(lin - logw)
        e_neg = jnp.exp(-lin)
        e_tot = jnp.exp(tot)
        rt = r * e_in
        at = kk * e_ex
        kh = kdir * e_neg
        bh = b * e_neg
        kc = kh * e_tot
        bc = bh * e_tot

        ys = []
        for h in range(H):
            sl = slice(h * N, (h + 1) * N)
            lhs2 = jnp.concatenate([at[:, sl], rt[:, sl]], axis=0)
            rhs2 = jnp.concatenate([kh[:, sl], bh[:, sl]], axis=0)
            g2 = _dot_nt(lhs2, rhs2)
            s0 = state_ref[d, h]
            w0s = _dot_nt(lhs2, s0)
            a_ak = jnp.where(strict, g2[0:C, 0:C], 0.0)
            a_ab = jnp.where(strict, g2[0:C, C:2 * C], 0.0)
            a_rk = jnp.where(incl, g2[C:2 * C, 0:C], 0.0)
            a_rb = jnp.where(incl, g2[C:2 * C, C:2 * C], 0.0)
            vh = v[:, sl]
            av = _dot(jnp.concatenate([a_ak, a_rk], axis=0), vh)
            u = w0s[0:C] + av[0:C]
            x = -a_ab
            npow = int(math.log2(C))
            for i in range(npow):
                u = u + _dot(x, u)
                if i < npow - 1:
                    x = _dot(x, x)
            yh = w0s[C:2 * C] + av[C:2 * C] - _dot(a_rb, u)
            upd = _dot_tn(jnp.concatenate([vh, u], axis=0),
                          jnp.concatenate([kc[:, sl], -bc[:, sl]], axis=0))
            state_ref[d, h] = s0 * e_tot[:, sl] + upd
            ys.append(yh)
        y = jnp.concatenate(ys, axis=1)

        base = pl.multiple_of(ci * C, C)

        @pl.when(c < nc // 2)
        def _():
            o_ref[0, pl.ds(base, C), :] = y

        @pl.when(c >= nc // 2)
        def _():
            yt = o_ref[0, pl.ds(base, C), :] + y
            inv_n = 1.0 / N
            mean = _dot_exact_rhs(yt, hsum) * inv_n
            yc = yt - mean
            var = _dot_exact_rhs(yc * yc, hsum) * inv_n
            yn = yc * lax.rsqrt(var + RWKV_GN_EPS) * gng_ref[...] + gnb_ref[...]
            coef = _dot_exact_rhs(r * rk_ref[...] * (kdir_both[0] + kdir_both[1]), hsum)
            g = _dot(_sigmoid(x1), gup_ref[...])
            o_ref[0, pl.ds(base, C), :] = (yn + coef * v) * g

    one_direction(0, c)
    one_direction(1, nc - 1 - c)


def _rwkv_mixer(p_pad, mu_pad, w0, w_up, a0, a_up, g_up, k_k, k_a, r_k, gn_g, gn_b):
    B, S, PW = p_pad.shape
    W = RWKV_WIDTH
    C = RWKV_CHUNK
    wup = jnp.zeros((2, LANES, W), F32)
    for d in range(2):
        lo = GATE_LORA + d * DECAY_LORA
        wup = wup.at[d, lo:lo + DECAY_LORA].set(w_up[d])
    aup = jnp.zeros((2, LANES, W), F32)
    for d in range(2):
        aup = aup.at[d, d * ICLR_LORA:(d + 1) * ICLR_LORA].set(a_up[d])
    gup = jnp.zeros((LANES, W), F32).at[:GATE_LORA].set(g_up)
    head = jnp.arange(W) // HEAD_DIM
    hsum = (head[:, None] == head[None, :]).astype(BF16)
    row = lambda t: t.reshape(1, W)
    const2 = lambda shape: pl.BlockSpec(shape, lambda b, c: (0,) * len(shape))
    return pl.pallas_call(
        functools.partial(_rwkv_kernel, seq=S, chunk=C),
        grid=(B, S // C // 1),
        in_specs=[pl.BlockSpec((1, S, PW), lambda b, c: (b, 0, 0)),
                  const2((1, PW)), const2((2, LANES, W)), const2((2, LANES, W)), const2((LANES, W)),
                  const2((2, W)), const2((2, W)), const2((1, W)), const2((1, W)), const2((1, W)),
                  const2((1, W)), const2((1, W)), const2((W, W))],
        out_specs=pl.BlockSpec((1, S, W), lambda b, c: (b, 0, 0)),
        out_shape=jax.ShapeDtypeStruct((B, S, W), F32),
        scratch_shapes=[pltpu.VMEM((2, RWKV_HEADS, HEAD_DIM, HEAD_DIM), F32)],
        compiler_params=pltpu.CompilerParams(dimension_semantics=("arbitrary", "arbitrary"),
                                             vmem_limit_bytes=VMEM_LIMIT),
        name="rwkv7_scan",
    )(p_pad, mu_pad.reshape(1, PW), _bf(wup), _bf(aup), _bf(gup), w0, a0, row(k_k), row(k_a),
      r_k.reshape(1, W), row(gn_g), row(gn_b), hsum)


N_EXPERTS = 32
TOP_K = 4
SWIGLU_LIMIT = 7.0
SWIGLU_ALPHA = 1.702
MOE_ROWS = 256


def _expert_kernel(be_ref, x_ref, wgu_ref, bgu_ref, wd_ref, bd_ref, o_ref):
    F = wd_ref.shape[1]
    gu = jnp.dot(x_ref[...], wgu_ref[0], preferred_element_type=F32) + bgu_ref[0]
    gate = jnp.minimum(gu[:, :F], SWIGLU_LIMIT)
    up = jnp.clip(gu[:, F:], -SWIGLU_LIMIT, SWIGLU_LIMIT)
    act = (up + 1.0) * gate * _sigmoid(SWIGLU_ALPHA * gate)
    o_ref[...] = jnp.dot(_bf(act), wd_ref[0], preferred_element_type=F32) + bd_ref[0]


def _expert_blocks(block_e, x_rows, w_gate_up, b_gate_up, w_down, b_down):
    n_rows, D = x_rows.shape
    E, _, F2 = w_gate_up.shape
    F = F2 // 2
    n_blocks = n_rows // MOE_ROWS
    return pl.pallas_call(
        _expert_kernel,
        grid_spec=pltpu.PrefetchScalarGridSpec(
            num_scalar_prefetch=1,
            grid=(n_blocks,),
            in_specs=[pl.BlockSpec((MOE_ROWS, D), lambda i, be: (i, 0)),
                      pl.BlockSpec((1, D, F2), lambda i, be: (be[i], 0, 0)),
                      pl.BlockSpec((1, 1, F2), lambda i, be: (be[i], 0, 0)),
                      pl.BlockSpec((1, F, D), lambda i, be: (be[i], 0, 0)),
                      pl.BlockSpec((1, 1, D), lambda i, be: (be[i], 0, 0))],
            out_specs=pl.BlockSpec((MOE_ROWS, D), lambda i, be: (i, 0))),
        out_shape=jax.ShapeDtypeStruct((n_rows, D), F32),
        compiler_params=pltpu.CompilerParams(dimension_semantics=("arbitrary",),
                                             vmem_limit_bytes=VMEM_LIMIT),
        name="moe_experts",
    )(block_e, x_rows, _bf(w_gate_up), b_gate_up.reshape(E, 1, F2), _bf(w_down), b_down.reshape(E, 1, D))


def _moe(h, w_router, b_router, w_gate_up, b_gate_up, w_down, b_down):
    T, D = h.shape
    n_assign = T * TOP_K
    logits = (h @ w_router).astype(F32) + b_router
    top_v, top_e = lax.top_k(logits, TOP_K)
    gates = jax.nn.softmax(top_v, axis=-1)
    flat_e = top_e.reshape(-1)
    order = jnp.argsort(flat_e)
    sorted_e = flat_e[order]
    counts = jnp.bincount(flat_e, length=N_EXPERTS)
    padded = (counts + MOE_ROWS - 1) // MOE_ROWS * MOE_ROWS
    starts = jnp.cumsum(counts) - counts
    pends = jnp.cumsum(padded)
    pstarts = pends - padded
    dest_sorted = pstarts[sorted_e] + jnp.arange(n_assign) - starts[sorted_e]
    dest = jnp.zeros((n_assign,), jnp.int32).at[order].set(dest_sorted.astype(jnp.int32))
    n_rows = n_assign + N_EXPERTS * MOE_ROWS
    n_blocks = n_rows // MOE_ROWS
    row_token = jnp.full((n_rows,), T, jnp.int32).at[dest].set(jnp.arange(n_assign, dtype=jnp.int32) // TOP_K)
    x_rows = jnp.concatenate([_bf(h), jnp.zeros((1, D), BF16)], axis=0)[row_token]
    block_e = jnp.minimum(jnp.searchsorted(pends, jnp.arange(n_blocks) * MOE_ROWS, side='right'),
                          N_EXPERTS - 1).astype(jnp.int32)
    y_rows = _expert_blocks(block_e, x_rows, w_gate_up, b_gate_up, w_down, b_down)
    return jnp.einsum('tkd,tk->td', y_rows[dest].reshape(T, TOP_K, D), gates)


LN_EPS = 1e-5
NEG_INF = -1e30
ATT_HEADS = 6
ATT_WIDTH = ATT_HEADS * HEAD_DIM
MEM_HEADS = 4
MEM_WIDTH = MEM_HEADS * HEAD_DIM
DILATED_BRANCHES = ((128, 1), (512, 4), (2048, 16))
RWKV_IN = 3 * RWKV_WIDTH + GATE_LORA + 2 * DECAY_LORA + 2 * ICLR_LORA


def _layer_norm(x, g, b):
    mu = jnp.mean(x, -1, keepdims=True)
    var = jnp.mean(jnp.square(x - mu), -1, keepdims=True)
    return (x - mu) * lax.rsqrt(var + LN_EPS) * g + b


def _dilated_branch(q, k, v, slopes, window, dil):
    B, H, S, N = q.shape
    half = window // (2 * dil)
    L = S // dil
    nb = -(-L // half)
    pad = nb * half - L

    def by_residue(t):
        return jnp.transpose(t.reshape(B, H, L, dil, N), (0, 1, 3, 2, 4))

    qb = jnp.pad(by_residue(q), ((0, 0), (0, 0), (0, 0), (0, pad), (0, 0))).reshape(B, H, dil, nb, half, N)

    def windows(t):
        t = jnp.pad(by_residue(t), ((0, 0), (0, 0), (0, 0), (half, pad + half), (0, 0))).reshape(B, H, dil, nb + 2, half, N)
        return jnp.concatenate([t[:, :, :, :-2], t[:, :, :, 1:-1], t[:, :, :, 2:]], axis=4)

    kw, vw = windows(k), windows(v)
    s = jnp.einsum('bhrnqd,bhrnkd->bhrnqk', qb, kw).astype(jnp.float32)
    qi = jnp.arange(half)[:, None]
    kj = jnp.arange(3 * half)[None, :]
    delta = kj - half - qi
    lk = jnp.arange(nb)[:, None, None] * half - half + kj[None]
    valid = (jnp.abs(delta) <= half)[None] & (lk >= 0) & (lk < L)
    dist = (jnp.abs(delta) * dil).astype(jnp.float32)
    s = jnp.where(valid, s - slopes[None, :, None, None, None, None] * dist, NEG_INF)
    m = jnp.max(s, -1, keepdims=True)
    pr = jnp.exp(s - m)
    den = jnp.sum(pr, -1, keepdims=True)
    o = jnp.einsum('bhrnqk,bhrnkd->bhrnqd', pr, vw) / den
    lse = (m + jnp.log(den))[..., 0]

    def back(t):
        t = t.reshape((B, H, dil, nb * half) + t.shape[5:])[:, :, :, :L]
        t = jnp.moveaxis(t, 2, 3)
        return t.reshape((B, H, S) + t.shape[4:])

    return back(o), back(lse)


def _dilated_mixture_attention(q, k, v):
    slopes = jnp.exp2(-8.0 * jnp.arange(1, ATT_HEADS + 1, dtype=jnp.float32) / ATT_HEADS)
    outs, lses = [], []
    for window, dil in DILATED_BRANCHES:
        o_b, lse_b = _dilated_branch(q, k, v, slopes, window, dil)
        outs.append(o_b)
        lses.append(lse_b)
    wts = jax.nn.softmax(jnp.stack(lses), axis=0)
    return jnp.sum(wts[..., None] * jnp.stack(outs), axis=0)


def kernel(x, mem, ln_in_g, ln_in_b, w_in, mu_shift, w0, w_up, a0, a_up, g_up, k_k, k_a, r_k, gn_g, gn_b, w_mem_kv, w_out, ln1_g, ln1_b, w_router, b_router, w_gate_up, b_gate_up, w_down, b_down, ln2_g, ln2_b):
    B, S, D = x.shape
    depth = w_in.shape[0]
    alpha = (2 * depth) ** 0.25
    h = _layer_norm(x, ln_in_g, ln_in_b)
    for l in range(depth):
        p = h @ w_in[l]
        pad = RWKV_PAD_IN - RWKV_IN
        p_pad = jnp.pad(p[..., :RWKV_IN], ((0, 0), (0, 0), (0, pad)))
        mu_pad = jnp.pad(mu_shift[l], (0, pad))
        y_rwkv = _rwkv_mixer(p_pad, mu_pad, w0[l], w_up[l], a0[l], a_up[l], g_up[l], k_k[l], k_a[l], r_k[l],
                             gn_g[l], gn_b[l])
        o = RWKV_IN

        def heads_first(t, n):
            return jnp.transpose(t.reshape(B, S, n, HEAD_DIM), (0, 2, 1, 3))

        q = heads_first(p[..., o:o + ATT_WIDTH], ATT_HEADS) * HEAD_DIM ** -0.5
        k = heads_first(p[..., o + ATT_WIDTH:o + 2 * ATT_WIDTH], ATT_HEADS)
        v = heads_first(p[..., o + 2 * ATT_WIDTH:o + 3 * ATT_WIDTH], ATT_HEADS)
        y_att = jnp.transpose(_dilated_mixture_attention(q, k, v), (0, 2, 1, 3)).reshape(B, S, ATT_WIDTH)
        o += 3 * ATT_WIDTH
        q_m = p[..., o:].reshape(B, S, MEM_HEADS, HEAD_DIM)
        kv_m = (mem @ w_mem_kv[l]).reshape(B, -1, 2, MEM_HEADS, HEAD_DIM)
        s_m = jnp.einsum('bshd,bmhd->bhsm', q_m, kv_m[:, :, 0]) * HEAD_DIM ** -0.5
        y_mem = jnp.einsum('bhsm,bmhd->bshd', jax.nn.softmax(s_m, axis=-1), kv_m[:, :, 1]).reshape(B, S, MEM_WIDTH)
        y = jnp.concatenate([y_rwkv, y_att, y_mem], axis=-1)
        h = _layer_norm(alpha * h + y @ w_out[l], ln1_g[l], ln1_b[l])
        ffn = _moe(h.reshape(B * S, D), w_router[l], b_router[l], w_gate_up[l], b_gate_up[l], w_down[l], b_down[l])
        h = _layer_norm(alpha * h + ffn.reshape(B, S, D), ln2_g[l], ln2_b[l])
    return h
```

```python
import functools
import math

import jax
import jax.numpy as jnp
from jax import lax
from jax.experimental import pallas as pl
from jax.experimental.pallas import tpu as pltpu

F32 = jnp.float32
BF16 = jnp.bfloat16

HEAD_DIM = 64
RWKV_HEADS = 6
RWKV_WIDTH = RWKV_HEADS * HEAD_DIM
GATE_LORA = 64
DECAY_LORA = 32
ICLR_LORA = 32
LANES = 128
RWKV_CHUNK = 64
RWKV_PAD_IN = 3 * RWKV_WIDTH + 2 * LANES
RWKV_GN_EPS = 64e-5
DECAY_SCALE = math.exp(-0.5)
VMEM_LIMIT = 56 * 1024 * 1024


def _bf(x):
    return x.astype(BF16)


def _dot(a, b):
    return jnp.dot(_bf(a), _bf(b), preferred_element_type=F32)


def _dot_nt(a, b):
    return lax.dot_general(_bf(a), _bf(b), (((1,), (1,)), ((), ())), preferred_element_type=F32)


def _dot_tn(a, b):
    return lax.dot_general(_bf(a), _bf(b), (((0,), (0,)), ((), ())), preferred_element_type=F32)


def _split3(x):
    hi = _bf(x)
    r1 = x - hi.astype(F32)
    mid = _bf(r1)
    lo = _bf(r1 - mid.astype(F32))
    return hi, mid, lo


def _dot_exact_lhs(a_bf, x, terms=3):
    parts = _split3(x)[:terms]
    acc = jnp.dot(a_bf, parts[0], preferred_element_type=F32)
    for p in parts[1:]:
        acc = acc + jnp.dot(a_bf, p, preferred_element_type=F32)
    return acc


def _dot_exact_rhs(x, b_bf, terms=2):
    parts = _split3(x)[:terms]
    acc = jnp.dot(parts[0], b_bf, preferred_element_type=F32)
    for p in parts[1:]:
        acc = acc + jnp.dot(p, b_bf, preferred_element_type=F32)
    return acc


def _sigmoid(x):
    return 1.0 / (1.0 + jnp.exp(-x))


def _rwkv_kernel(p_ref, mu_ref, wup_ref, aup_ref, gup_ref, w0_ref, a0_ref, kk_ref, ka_ref, rk_ref,
                 gng_ref, gnb_ref, hsum_ref, o_ref, state_ref, *, seq, chunk):
    C, N, H, W = chunk, HEAD_DIM, RWKV_HEADS, RWKV_WIDTH
    nc = seq // C
    c = pl.program_id(1)

    @pl.when(c == 0)
    def _():
        state_ref[...] = jnp.zeros_like(state_ref)

    hsum = hsum_ref[...]
    row = lax.broadcasted_iota(jnp.int32, (C, C), 0)
    col = lax.broadcasted_iota(jnp.int32, (C, C), 1)
    rowc = lax.broadcasted_iota(jnp.int32, (C, 1), 0)

    def shifted(ci):
        base = pl.multiple_of(ci * C, C)
        cur = p_ref[0, pl.ds(base, C), :]
        pb = p_ref[0, pl.ds(pl.multiple_of(jnp.maximum(base - 8, 0), 8), 8), :]
        nb = p_ref[0, pl.ds(pl.multiple_of(jnp.minimum(base + C, seq - 8), 8), 8), :]
        prev_row = jnp.where(ci > 0, pb[7:8, :], 0.0)
        next_row = jnp.where(ci < nc - 1, nb[0:1, :], 0.0)
        prev = jnp.where(rowc == 0, prev_row, pltpu.roll(cur, 1, 0))
        nxt = jnp.where(rowc == C - 1, next_row, pltpu.roll(cur, C - 1, 0))
        return cur + mu_ref[...] * (0.5 * (prev + nxt) - cur)

    def prepare(d, ci):
        ps = shifted(ci)
        r = ps[:, 0:W]
        k = ps[:, W:2 * W]
        v = ps[:, 2 * W:3 * W]
        x1 = ps[:, 3 * W:3 * W + LANES]
        x2 = ps[:, 3 * W + LANES:3 * W + 2 * LANES]
        a_both = [_sigmoid(a0_ref[e:e + 1, :] + _dot(x2, aup_ref[e])) for e in range(2)]
        w_logit = w0_ref[d:d + 1, :] + _dot(jnp.tanh(x1), wup_ref[d])
        logw = -DECAY_SCALE * _sigmoid(w_logit)
        kkr = k * kk_ref[...]
        ss = _dot_exact_rhs(kkr * kkr, hsum)
        kk = kkr / jnp.maximum(jnp.sqrt(ss), 1e-12)
        kdir_both = [k * (1.0 + (a - 1.0) * ka_ref[...]) for a in a_both]
        kdir = kdir_both[d]
        b = kk * a_both[d]
        incl = (row >= col) if d == 0 else (row <= col)
        tri = jnp.where(incl, 1.0, 0.0).astype(BF16)
        lin = _dot_exact_lhs(tri, logw)
        tot = jnp.sum(logw, axis=0, keepdims=True)
        e_neg = jnp.exp(-lin)
        e_tot = jnp.exp(tot)
        kh = kdir * e_neg
        bh = b * e_neg
        return dict(r=r, v=v, x1=x1, ksum=kdir_both[0] + kdir_both[1], rt=r * jnp.exp(lin),
                    at=kk * jnp.exp(lin - logw), kh=kh, bh=bh, kc=kh * e_tot, bc=bh * e_tot, e_tot=e_tot)

    chunk_of = (c, nc - 1 - c)
    pre = [prepare(d, chunk_of[d]) for d in range(2)]
    chains = [(d, h) for d in range(2) for h in range(H)]
    incl_m = (row >= col, row <= col)
    strict_m = (row > col, row < col)

    lhs2, s0, g2, w0s = {}, {}, {}, {}
    for d, h in chains:
        sl = slice(h * N, (h + 1) * N)
        q = pre[d]
        lhs2[d, h] = jnp.concatenate([q['at'][:, sl], q['rt'][:, sl]], axis=0)
        rhs2 = jnp.concatenate([q['kh'][:, sl], q['bh'][:, sl]], axis=0)
        g2[d, h] = _dot_nt(lhs2[d, h], rhs2)
        s0[d, h] = state_ref[d, h]
        w0s[d, h] = _dot_nt(lhs2[d, h], s0[d, h])
    av, x, u, a_rb = {}, {}, {}, {}
    for d, h in chains:
        sl = slice(h * N, (h + 1) * N)
        g = g2[d, h]
        a_ak = jnp.where(strict_m[d], g[0:C, 0:C], 0.0)
        a_rk = jnp.where(incl_m[d], g[C:2 * C, 0:C], 0.0)
        x[d, h] = jnp.where(strict_m[d], -g[0:C, C:2 * C], 0.0)
        a_rb[d, h] = jnp.where(incl_m[d], g[C:2 * C, C:2 * C], 0.0)
        av[d, h] = _dot(jnp.concatenate([a_ak, a_rk], axis=0), pre[d]['v'][:, sl])
    for d, h in chains:
        u[d, h] = w0s[d, h][0:C] + av[d, h][0:C]
    npow = int(math.log2(C))
    for i in range(npow):
        for d, h in chains:
            u[d, h] = u[d, h] + _dot(x[d, h], u[d, h])
        if i < npow - 1:
            for d, h in chains:
                x[d, h] = _dot(x[d, h], x[d, h])
    ys = {}
    for d, h in chains:
        sl = slice(h * N, (h + 1) * N)
        q = pre[d]
        ys[d, h] = w0s[d, h][C:2 * C] + av[d, h][C:2 * C] - _dot(a_rb[d, h], u[d, h])
        upd = _dot_tn(jnp.concatenate([q['v'][:, sl], u[d, h]], axis=0),
                      jnp.concatenate([q['kc'][:, sl], -q['bc'][:, sl]], axis=0))
        state_ref[d, h] = s0[d, h] * q['e_tot'][:, sl] + upd

    for d in range(2):
        q = pre[d]
        y = jnp.concatenate([ys[d, h] for h in range(H)], axis=1)
        base = pl.multiple_of(chunk_of[d] * C, C)

        @pl.when(c < nc // 2)
        def _():
            o_ref[0, pl.ds(base, C), :] = y

        @pl.when(c >= nc // 2)
        def _():
            yt = o_ref[0, pl.ds(base, C), :] + y
            inv_n = 1.0 / N
            mean = _dot_exact_rhs(yt, hsum) * inv_n
            yc = yt - mean
            var = _dot_exact_rhs(yc * yc, hsum) * inv_n
            yn = yc * lax.rsqrt(var + RWKV_GN_EPS) * gng_ref[...] + gnb_ref[...]
            coef = _dot_exact_rhs(q['r'] * rk_ref[...] * q['ksum'], hsum)
            gate = _dot(_sigmoid(q['x1']), gup_ref[...])
            o_ref[0, pl.ds(base, C), :] = (yn + coef * q['v']) * gate


def _rwkv_mixer(p_pad, mu_pad, w0, w_up, a0, a_up, g_up, k_k, k_a, r_k, gn_g, gn_b):
    B, S, PW = p_pad.shape
    W = RWKV_WIDTH
    C = RWKV_CHUNK
    wup = jnp.zeros((2, LANES, W), F32)
    for d in range(2):
        lo = GATE_LORA + d * DECAY_LORA
        wup = wup.at[d, lo:lo + DECAY_LORA].set(w_up[d])
    aup = jnp.zeros((2, LANES, W), F32)
    for d in range(2):
        aup = aup.at[d, d * ICLR_LORA:(d + 1) * ICLR_LORA].set(a_up[d])
    gup = jnp.zeros((LANES, W), F32).at[:GATE_LORA].set(g_up)
    head = jnp.arange(W) // HEAD_DIM
    hsum = (head[:, None] == head[None, :]).astype(BF16)
    row = lambda t: t.reshape(1, W)
    const2 = lambda shape: pl.BlockSpec(shape, lambda b, c: (0,) * len(shape))
    return pl.pallas_call(
        functools.partial(_rwkv_kernel, seq=S, chunk=C),
        grid=(B, S // C // 1),
        in_specs=[pl.BlockSpec((1, S, PW), lambda b, c: (b, 0, 0)),
                  const2((1, PW)), const2((2, LANES, W)), const2((2, LANES, W)), const2((LANES, W)),
                  const2((2, W)), const2((2, W)), const2((1, W)), const2((1, W)), const2((1, W)),
                  const2((1, W)), const2((1, W)), const2((W, W))],
        out_specs=pl.BlockSpec((1, S, W), lambda b, c: (b, 0, 0)),
        out_shape=jax.ShapeDtypeStruct((B, S, W), F32),
        scratch_shapes=[pltpu.VMEM((2, RWKV_HEADS, HEAD_DIM, HEAD_DIM), F32)],
        compiler_params=pltpu.CompilerParams(dimension_semantics=("arbitrary", "arbitrary"),
                                             vmem_limit_bytes=VMEM_LIMIT),
        name="rwkv7_scan",
    )(p_pad, mu_pad.reshape(1, PW), _bf(wup), _bf(aup), _bf(gup), w0, a0, row(k_k), row(k_a),
      r_k.reshape(1, W), row(gn_g), row(gn_b), hsum)


ATT_HEADS = 6
ATT_WIDTH = ATT_HEADS * HEAD_DIM
DILATED_BRANCHES = ((128, 1), (512, 4), (2048, 16))
ATT_BLOCK = 64
NEG_INF = -1e30
ATT_UNROLL = 8


def _rows(start, size, stride):
    return pl.ds(start, size) if stride == 1 else pl.ds(start, size, stride=stride)


def _att_kernel(slope_ref, q_ref, k_ref, v_ref, out_ref, o_s, m_s, l_s, *, seq):
    S, Q = seq, ATT_BLOCK
    left = lax.broadcasted_iota(jnp.int32, (1, LANES), 1) < HEAD_DIM
    qi = lax.broadcasted_iota(jnp.int32, (Q, 3 * Q), 0)
    kj = lax.broadcasted_iota(jnp.int32, (Q, 3 * Q), 1)
    dist = jnp.abs(kj - Q - qi).astype(F32)
    band = dist <= Q
    for bi, (window, dil) in enumerate(DILATED_BRANCHES):
        assert window // (2 * dil) == Q and S % (Q * dil) == 0
        nb = S // (Q * dil)
        bias = [jnp.where(band, (-dil * slope_ref[0, hh:hh + 1, 0:1]) * dist, NEG_INF) for hh in range(2)]

        def body(it, carry, dil=dil, nb=nb, bias=bias, bi=bi):
            blocks = []
            for uu in range(ATT_UNROLL):
                i = it * ATT_UNROLL + uu
                n = i % nb
                start = pl.multiple_of(i * Q, Q) if dil == 1 else i // nb + dil * Q * n
                s_prev = start - dil * Q * (n > 0).astype(jnp.int32)
                s_next = start + dil * Q * (n < nb - 1).astype(jnp.int32)
                if dil == 1:
                    s_prev, s_next = pl.multiple_of(s_prev, Q), pl.multiple_of(s_next, Q)
                q = q_ref[0, _rows(start, Q, dil), :] * (HEAD_DIM ** -0.5)
                kw = _bf(jnp.concatenate([k_ref[0, _rows(s, Q, dil), :] for s in (s_prev, start, s_next)], axis=0))
                vw = _bf(jnp.concatenate([v_ref[0, _rows(s, Q, dil), :] for s in (s_prev, start, s_next)], axis=0))
                pen_prev = jnp.where(n == 0, NEG_INF, 0.0)
                pen_next = jnp.where(n == nb - 1, NEG_INF, 0.0)
                edge = jnp.where(kj < Q, pen_prev, jnp.where(kj >= 2 * Q, pen_next, 0.0))
                blocks.append((start, q, kw, vw, edge))
            scores = [[_dot_nt(jnp.where(left if hh == 0 else jnp.logical_not(left), q, 0.0), kw) + bias[hh] + edge
                       for hh in range(2)] for (_, q, kw, _, edge) in blocks]
            mx = [[jnp.max(s, axis=-1, keepdims=True) for s in ss] for ss in scores]
            pr = [[jnp.exp(s - m) for s, m in zip(ss, ms)] for ss, ms in zip(scores, mx)]
            den = [[jnp.sum(p, axis=-1, keepdims=True) for p in ps] for ps in pr]
            pv = [[jnp.dot(_bf(p), blk[3], preferred_element_type=F32) for p in ps] for ps, blk in zip(pr, blocks)]
            for uu, blk in enumerate(blocks):
                rows = _rows(blk[0], Q, dil)
                o_s[bi, rows, :] = jnp.where(left, pv[uu][0], pv[uu][1])
                m_s[bi, rows, :] = jnp.where(left, mx[uu][0], mx[uu][1])
                l_s[bi, rows, :] = jnp.where(left, den[uu][0], den[uu][1])
            return carry

        lax.fori_loop(0, S // Q // ATT_UNROLL, body, 0)

    def merge(j, carry):
        rows = pl.ds(pl.multiple_of(j * Q, Q), Q)
        ms = [m_s[b, rows, :] for b in range(3)]
        top = jnp.maximum(jnp.maximum(ms[0], ms[1]), ms[2])
        ws = [jnp.exp(m - top) for m in ms]
        num = ws[0] * o_s[0, rows, :] + ws[1] * o_s[1, rows, :] + ws[2] * o_s[2, rows, :]
        den = ws[0] * l_s[0, rows, :] + ws[1] * l_s[1, rows, :] + ws[2] * l_s[2, rows, :]
        out_ref[0, rows, :] = num / den
        return carry

    lax.fori_loop(0, S // Q, merge, 0)


def _dilated_attention(q, k, v):
    B, S, W = q.shape
    pairs = W // LANES
    slopes = jnp.exp2(-8.0 * jnp.arange(1, ATT_HEADS + 1, dtype=F32) / ATT_HEADS).reshape(pairs, 2, 1)
    slopes = jnp.broadcast_to(jnp.pad(slopes, ((0, 0), (0, 6), (0, 0))), (pairs, 8, LANES))
    spec = pl.BlockSpec((1, S, LANES), lambda b, p: (b, 0, p))
    return pl.pallas_call(
        functools.partial(_att_kernel, seq=S),
        grid=(B, pairs),
        in_specs=[pl.BlockSpec((1, 8, LANES), lambda b, p: (p, 0, 0)), spec, spec, spec],
        out_specs=spec,
        out_shape=jax.ShapeDtypeStruct((B, S, W), F32),
        scratch_shapes=[pltpu.VMEM((3, S, LANES), F32)] * 3,
        compiler_params=pltpu.CompilerParams(dimension_semantics=("arbitrary", "arbitrary"),
                                             vmem_limit_bytes=VMEM_LIMIT),
        name="dilated_attention",
    )(slopes, q, k, v)


N_EXPERTS = 32
TOP_K = 4
SWIGLU_LIMIT = 7.0
SWIGLU_ALPHA = 1.702
MOE_ROWS = 256


def _expert_kernel(be_ref, x_ref, wgu_ref, bgu_ref, wd_ref, bd_ref, o_ref):
    F = wd_ref.shape[1]
    gu = jnp.dot(x_ref[...], wgu_ref[0], preferred_element_type=F32) + bgu_ref[0]
    gate = jnp.minimum(gu[:, :F], SWIGLU_LIMIT)
    up = jnp.clip(gu[:, F:], -SWIGLU_LIMIT, SWIGLU_LIMIT)
    act = (up + 1.0) * gate * _sigmoid(SWIGLU_ALPHA * gate)
    o_ref[...] = jnp.dot(_bf(act), wd_ref[0], preferred_element_type=F32) + bd_ref[0]


def _expert_blocks(block_e, x_rows, w_gate_up, b_gate_up, w_down, b_down):
    n_rows, D = x_rows.shape
    E, _, F2 = w_gate_up.shape
    F = F2 // 2
    n_blocks = n_rows // MOE_ROWS
    return pl.pallas_call(
        _expert_kernel,
        grid_spec=pltpu.PrefetchScalarGridSpec(
            num_scalar_prefetch=1,
            grid=(n_blocks,),
            in_specs=[pl.BlockSpec((MOE_ROWS, D), lambda i, be: (i, 0)),
                      pl.BlockSpec((1, D, F2), lambda i, be: (be[i], 0, 0)),
                      pl.BlockSpec((1, 1, F2), lambda i, be: (be[i], 0, 0)),
                      pl.BlockSpec((1, F, D), lambda i, be: (be[i], 0, 0)),
                      pl.BlockSpec((1, 1, D), lambda i, be: (be[i], 0, 0))],
            out_specs=pl.BlockSpec((MOE_ROWS, D), lambda i, be: (i, 0))),
        out_shape=jax.ShapeDtypeStruct((n_rows, D), F32),
        compiler_params=pltpu.CompilerParams(dimension_semantics=("arbitrary",),
                                             vmem_limit_bytes=VMEM_LIMIT),
        name="moe_experts",
    )(block_e, x_rows, _bf(w_gate_up), b_gate_up.reshape(E, 1, F2), _bf(w_down), b_down.reshape(E, 1, D))


def _route(top_e, n_tokens):
    T = n_tokens
    n_rows = T * TOP_K + N_EXPERTS * MOE_ROWS
    n_blocks = n_rows // MOE_ROWS
    experts = jnp.arange(N_EXPERTS, dtype=jnp.int32)
    onehot = jnp.sum((top_e[:, :, None] == experts).astype(jnp.int32), axis=1)
    incl = jnp.cumsum(onehot, axis=0)
    counts = incl[-1]
    padded = (counts + MOE_ROWS - 1) // MOE_ROWS * MOE_ROWS
    starts = jnp.cumsum(counts) - counts
    pends = jnp.cumsum(padded)
    pstarts = pends - padded
    rank = jnp.take_along_axis(incl - onehot, top_e, axis=1)
    dest = pstarts[top_e] + rank
    order = jnp.argsort(top_e.reshape(-1), stable=True).astype(jnp.int32)
    block_e = jnp.minimum(jnp.searchsorted(pends, jnp.arange(n_blocks, dtype=jnp.int32) * MOE_ROWS, side='right'),
                          N_EXPERTS - 1).astype(jnp.int32)
    row = jnp.arange(n_rows, dtype=jnp.int32)
    row_e = jnp.repeat(block_e, MOE_ROWS)
    within = row - pstarts[row_e]
    src = jnp.clip(within + starts[row_e], 0, T * TOP_K - 1)
    row_token = jnp.where(within < counts[row_e], order[src] // TOP_K, T).astype(jnp.int32)
    return dest.astype(jnp.int32), row_token, block_e


LN_EPS = 1e-5
MEM_HEADS = 4
MEM_WIDTH = MEM_HEADS * HEAD_DIM
RWKV_IN = 3 * RWKV_WIDTH + GATE_LORA + 2 * DECAY_LORA + 2 * ICLR_LORA
PROJ_ROWS = 512


def _ln(x, g, b):
    mu = jnp.mean(x, -1, keepdims=True)
    xc = x - mu
    var = jnp.mean(xc * xc, -1, keepdims=True)
    return xc * lax.rsqrt(var + LN_EPS) * g + b


def _in_proj_kernel(x_ref, g_ref, b_ref, w_ref, rw_ref, q_ref, k_ref, v_ref, qm_ref):
    hb = _bf(_ln(x_ref[...], g_ref[...], b_ref[...]))
    o = 0
    for ref in (rw_ref, q_ref, k_ref, v_ref, qm_ref):
        n = ref.shape[-1]
        ref[...] = jnp.dot(hb, w_ref[:, o:o + n], preferred_element_type=F32)
        o += n


def _in_proj(xt, ln_g, ln_b, w_pad):
    T, D = xt.shape
    widths = (RWKV_PAD_IN, ATT_WIDTH, ATT_WIDTH, ATT_WIDTH, MEM_WIDTH)
    assert sum(widths) == w_pad.shape[1]
    rows = lambda n: pl.BlockSpec((PROJ_ROWS, n), lambda i: (i, 0))
    full = lambda a: pl.BlockSpec(a.shape, lambda i: (0,) * a.ndim)
    args = (xt, ln_g.reshape(1, D), ln_b.reshape(1, D), w_pad)
    return pl.pallas_call(
        _in_proj_kernel,
        grid=(T // PROJ_ROWS,),
        in_specs=[rows(D)] + [full(a) for a in args[1:]],
        out_specs=[rows(n) for n in widths],
        out_shape=[jax.ShapeDtypeStruct((T, n), F32) for n in widths],
        compiler_params=pltpu.CompilerParams(dimension_semantics=("arbitrary",), vmem_limit_bytes=VMEM_LIMIT),
        name="in_proj",
    )(*args)


def _mem_att_kernel(q_ref, mem_ref, wkv_ref, o_ref, kv_s):
    W = MEM_WIDTH

    @pl.when(pl.program_id(1) == 0)
    def _():
        kv_s[...] = _bf(jnp.dot(_bf(mem_ref[0]), wkv_ref[...], preferred_element_type=F32))

    q = q_ref[0] * (HEAD_DIM ** -0.5)
    lane = lax.broadcasted_iota(jnp.int32, (1, W), 1)
    kmat = kv_s[:, 0:W]
    vmat = kv_s[:, W:2 * W]
    out = jnp.zeros(q.shape, F32)
    for h in range(MEM_HEADS):
        sel = (lane >= h * HEAD_DIM) & (lane < (h + 1) * HEAD_DIM)
        s = lax.dot_general(_bf(jnp.where(sel, q, 0.0)), kmat, (((1,), (1,)), ((), ())),
                            preferred_element_type=F32)
        p = jnp.exp(s - jnp.max(s, axis=-1, keepdims=True))
        pv = jnp.dot(_bf(p), vmat, preferred_element_type=F32)
        out = jnp.where(sel, pv / jnp.sum(p, axis=-1, keepdims=True), out)
    o_ref[0] = out


def _mem_attention(q_m, mem, w_kv):
    B, S, W = q_m.shape
    _, M, D = mem.shape
    return pl.pallas_call(
        _mem_att_kernel,
        grid=(B, S // PROJ_ROWS),
        in_specs=[pl.BlockSpec((1, PROJ_ROWS, W), lambda b, i: (b, i, 0)),
                  pl.BlockSpec((1, M, D), lambda b, i: (b, 0, 0)),
                  pl.BlockSpec((D, 2 * W), lambda b, i: (0, 0))],
        out_specs=pl.BlockSpec((1, PROJ_ROWS, W), lambda b, i: (b, i, 0)),
        out_shape=jax.ShapeDtypeStruct((B, S, W), F32),
        scratch_shapes=[pltpu.VMEM((M, 2 * W), BF16)],
        compiler_params=pltpu.CompilerParams(dimension_semantics=("arbitrary", "arbitrary"),
                                             vmem_limit_bytes=VMEM_LIMIT),
        name="memory_attention",
    )(q_m, mem, _bf(w_kv))


def _out_proj_kernel(x_ref, yr_ref, ya_ref, ym_ref, g0_ref, b0_ref, w_ref, g1_ref, b1_ref, wrh_ref, wrl_ref,
                     br_ref, h1_ref, h1b_ref, gate_ref, expert_ref, *, alpha):
    h0 = _ln(x_ref[...], g0_ref[...], b0_ref[...])
    o = 0
    mix = None
    for ref in (yr_ref, ya_ref, ym_ref):
        n = ref.shape[-1]
        part = jnp.dot(_bf(ref[...]), w_ref[o:o + n, :], preferred_element_type=F32)
        mix = part if mix is None else mix + part
        o += n
    h1 = _ln(alpha * h0 + mix, g1_ref[...], b1_ref[...])
    h1_ref[...] = h1
    hi = _bf(h1)
    h1b_ref[...] = hi
    lo = _bf(h1 - hi.astype(F32))
    logits = (jnp.dot(hi, wrh_ref[...], preferred_element_type=F32)
              + jnp.dot(lo, wrh_ref[...], preferred_element_type=F32)
              + jnp.dot(hi, wrl_ref[...], preferred_element_type=F32)) + br_ref[...]
    lane = lax.broadcasted_iota(jnp.int32, logits.shape, 1).astype(F32)
    vals = logits
    top_v, top_i = [], []
    for _ in range(TOP_K):
        m = jnp.max(vals, axis=-1, keepdims=True)
        idx = jnp.min(jnp.where(vals == m, lane, float(LANES)), axis=-1, keepdims=True)
        top_v.append(m)
        top_i.append(idx)
        vals = jnp.where(lane == idx, -jnp.inf, vals)
    ex = [jnp.exp(v - top_v[0]) for v in top_v]
    den = ex[0] + ex[1] + ex[2] + ex[3]
    gates = jnp.zeros(logits.shape, F32)
    experts = jnp.zeros(logits.shape, F32)
    for k in range(TOP_K):
        gates = jnp.where(lane == k, ex[k] / den, gates)
        experts = jnp.where(lane == k, top_i[k], experts)
    gate_ref[...] = gates
    expert_ref[...] = experts.astype(jnp.int32)


def _out_proj(xt, y_rwkv, y_att, y_mem, ln0_g, ln0_b, w_out, ln1_g, ln1_b, w_router, b_router, alpha):
    T, D = xt.shape
    E = w_router.shape[1]
    wr = jnp.pad(w_router, ((0, 0), (0, LANES - E)))
    wr_hi = _bf(wr)
    wr_lo = _bf(wr - wr_hi.astype(F32))
    br = jnp.pad(b_router, (0, LANES - E), constant_values=NEG_INF).reshape(1, LANES)
    rows = lambda n: pl.BlockSpec((PROJ_ROWS, n), lambda i: (i, 0))
    full = lambda a: pl.BlockSpec(a.shape, lambda i: (0,) * a.ndim)
    vec = lambda t: t.reshape(1, D)
    consts = (vec(ln0_g), vec(ln0_b), _bf(w_out), vec(ln1_g), vec(ln1_b), wr_hi, wr_lo, br)
    return pl.pallas_call(
        functools.partial(_out_proj_kernel, alpha=alpha),
        grid=(T // PROJ_ROWS,),
        in_specs=[rows(D), rows(y_rwkv.shape[1]), rows(y_att.shape[1]), rows(y_mem.shape[1])]
                 + [full(a) for a in consts],
        out_specs=[rows(D), rows(D), rows(LANES), rows(LANES)],
        out_shape=[jax.ShapeDtypeStruct((T, D), F32), jax.ShapeDtypeStruct((T, D), BF16),
                   jax.ShapeDtypeStruct((T, LANES), F32), jax.ShapeDtypeStruct((T, LANES), jnp.int32)],
        compiler_params=pltpu.CompilerParams(dimension_semantics=("arbitrary",), vmem_limit_bytes=VMEM_LIMIT),
        name="out_proj_router",
    )(xt, y_rwkv, y_att, y_mem, *consts)


def _combine_kernel(h1_ref, y_ref, gate_ref, g_ref, b_ref, o_ref, *, alpha):
    acc = alpha * h1_ref[...]
    gates = gate_ref[...]
    for k in range(TOP_K):
        acc = acc + gates[:, k:k + 1] * y_ref[k]
    o_ref[...] = _ln(acc, g_ref[...], b_ref[...])


def _combine(h1, y_top, gates, ln_g, ln_b, alpha):
    T, D = h1.shape
    rows = lambda n: pl.BlockSpec((PROJ_ROWS, n), lambda i: (i, 0))
    full = lambda a: pl.BlockSpec(a.shape, lambda i: (0,) * a.ndim)
    consts = (ln_g.reshape(1, D), ln_b.reshape(1, D))
    return pl.pallas_call(
        functools.partial(_combine_kernel, alpha=alpha),
        grid=(T // PROJ_ROWS,),
        in_specs=[rows(D), pl.BlockSpec((TOP_K, PROJ_ROWS, D), lambda i: (0, i, 0)), rows(LANES)]
                 + [full(a) for a in consts],
        out_specs=rows(D),
        out_shape=jax.ShapeDtypeStruct((T, D), F32),
        compiler_params=pltpu.CompilerParams(dimension_semantics=("arbitrary",), vmem_limit_bytes=VMEM_LIMIT),
        name="combine_ln",
    )(h1, y_top, gates, *consts)


def kernel(x, mem, ln_in_g, ln_in_b, w_in, mu_shift, w0, w_up, a0, a_up, g_up, k_k, k_a, r_k, gn_g, gn_b, w_mem_kv, w_out, ln1_g, ln1_b, w_router, b_router, w_gate_up, b_gate_up, w_down, b_down, ln2_g, ln2_b):
    B, S, D = x.shape
    T = B * S
    depth = w_in.shape[0]
    assert depth == 1, "the layer norm feeding a layer is fused into its projection kernels"
    alpha = (2 * depth) ** 0.25
    l = 0
    xt = x.reshape(T, D)
    pad = RWKV_PAD_IN - RWKV_IN
    w_pad = _bf(jnp.concatenate([w_in[l][:, :RWKV_IN], jnp.zeros((D, pad), F32), w_in[l][:, RWKV_IN:]], axis=1))
    mu_pad = jnp.pad(mu_shift[l], (0, pad))
    p_rwkv, q_a, k_a_, v_a, q_m = _in_proj(xt, ln_in_g, ln_in_b, w_pad)
    seq = lambda t: t.reshape(B, S, t.shape[-1])
    y_rwkv = _rwkv_mixer(seq(p_rwkv), mu_pad, w0[l], w_up[l], a0[l], a_up[l], g_up[l], k_k[l], k_a[l], r_k[l],
                         gn_g[l], gn_b[l])
    y_att = _dilated_attention(seq(q_a), seq(k_a_), seq(v_a))
    y_mem = _mem_attention(seq(q_m), mem, w_mem_kv[l])
    flat = lambda t: t.reshape(T, t.shape[-1])
    h1, h1b, gates, experts = _out_proj(xt, flat(y_rwkv), flat(y_att), flat(y_mem), ln_in_g, ln_in_b, w_out[l],
                                        ln1_g[l], ln1_b[l], w_router[l], b_router[l], alpha)
    dest, row_token, block_e = _route(experts[:, :TOP_K], T)
    x_rows = jnp.take(jnp.concatenate([h1b, jnp.zeros((8, D), BF16)], axis=0), row_token, axis=0)
    y_rows = _expert_blocks(block_e, x_rows, w_gate_up[l], b_gate_up[l], w_down[l], b_down[l])
    y_top = jnp.take(y_rows, dest.T.reshape(-1), axis=0).reshape(TOP_K, T, D)
    out = _combine(h1, y_top, gates, ln2_g[l], ln2_b[l], alpha)
    return out.reshape(B, S, D)
```

```python
import functools
import math

import jax
import jax.numpy as jnp
from jax import lax
from jax.experimental import pallas as pl
from jax.experimental.pallas import tpu as pltpu

F32 = jnp.float32
BF16 = jnp.bfloat16

HEAD_DIM = 64
RWKV_HEADS = 6
RWKV_WIDTH = RWKV_HEADS * HEAD_DIM
GATE_LORA = 64
DECAY_LORA = 32
ICLR_LORA = 32
LANES = 128
RWKV_CHUNK = 64
RWKV_PAD_IN = 3 * RWKV_WIDTH + 2 * LANES
RWKV_GN_EPS = 64e-5
DECAY_SCALE = math.exp(-0.5)
VMEM_LIMIT = 56 * 1024 * 1024


def _bf(x):
    return x.astype(BF16)


def _dot(a, b):
    return jnp.dot(_bf(a), _bf(b), preferred_element_type=F32)


def _dot_nt(a, b):
    return lax.dot_general(_bf(a), _bf(b), (((1,), (1,)), ((), ())), preferred_element_type=F32)


def _dot_tn(a, b):
    return lax.dot_general(_bf(a), _bf(b), (((0,), (0,)), ((), ())), preferred_element_type=F32)


def _split3(x):
    hi = _bf(x)
    r1 = x - hi.astype(F32)
    mid = _bf(r1)
    lo = _bf(r1 - mid.astype(F32))
    return hi, mid, lo


def _dot_exact_lhs(a_bf, x, terms=3):
    parts = _split3(x)[:terms]
    acc = jnp.dot(a_bf, parts[0], preferred_element_type=F32)
    for p in parts[1:]:
        acc = acc + jnp.dot(a_bf, p, preferred_element_type=F32)
    return acc


def _dot_exact_rhs(x, b_bf, terms=2):
    parts = _split3(x)[:terms]
    acc = jnp.dot(parts[0], b_bf, preferred_element_type=F32)
    for p in parts[1:]:
        acc = acc + jnp.dot(p, b_bf, preferred_element_type=F32)
    return acc


def _sigmoid(x):
    return 1.0 / (1.0 + jnp.exp(-x))


def _rwkv_kernel(p_ref, mu_ref, wup_ref, aup_ref, gup_ref, w0_ref, a0_ref, kk_ref, ka_ref, rk_ref,
                 gng_ref, gnb_ref, hsum_ref, o_ref, state_ref, *, seq, chunk):
    C, N, H, W = chunk, HEAD_DIM, RWKV_HEADS, RWKV_WIDTH
    nc = seq // C
    c = pl.program_id(1)

    @pl.when(c == 0)
    def _():
        state_ref[...] = jnp.zeros_like(state_ref)

    hsum = hsum_ref[...]
    row = lax.broadcasted_iota(jnp.int32, (C, C), 0)
    col = lax.broadcasted_iota(jnp.int32, (C, C), 1)
    rowc = lax.broadcasted_iota(jnp.int32, (C, 1), 0)

    def shifted(ci):
        base = pl.multiple_of(ci * C, C)
        cur = p_ref[0, pl.ds(base, C), :]
        pb = p_ref[0, pl.ds(pl.multiple_of(jnp.maximum(base - 8, 0), 8), 8), :]
        nb = p_ref[0, pl.ds(pl.multiple_of(jnp.minimum(base + C, seq - 8), 8), 8), :]
        prev_row = jnp.where(ci > 0, pb[7:8, :], 0.0)
        next_row = jnp.where(ci < nc - 1, nb[0:1, :], 0.0)
        prev = jnp.where(rowc == 0, prev_row, pltpu.roll(cur, 1, 0))
        nxt = jnp.where(rowc == C - 1, next_row, pltpu.roll(cur, C - 1, 0))
        return cur + mu_ref[...] * (0.5 * (prev + nxt) - cur)

    def prepare(d, ci):
        ps = shifted(ci)
        r = ps[:, 0:W]
        k = ps[:, W:2 * W]
        v = ps[:, 2 * W:3 * W]
        x1 = ps[:, 3 * W:3 * W + LANES]
        x2 = ps[:, 3 * W + LANES:3 * W + 2 * LANES]
        a_both = [_sigmoid(a0_ref[e:e + 1, :] + _dot(x2, aup_ref[e])) for e in range(2)]
        w_logit = w0_ref[d:d + 1, :] + _dot(jnp.tanh(x1), wup_ref[d])
        logw = -DECAY_SCALE * _sigmoid(w_logit)
        kkr = k * kk_ref[...]
        ss = _dot_exact_rhs(kkr * kkr, hsum)
        kk = kkr / jnp.maximum(jnp.sqrt(ss), 1e-12)
        kdir_both = [k * (1.0 + (a - 1.0) * ka_ref[...]) for a in a_both]
        kdir = kdir_both[d]
        b = kk * a_both[d]
        incl = (row >= col) if d == 0 else (row <= col)
        tri = jnp.where(incl, 1.0, 0.0).astype(BF16)
        lin = _dot_exact_lhs(tri, logw)
        tot = jnp.sum(logw, axis=0, keepdims=True)
        e_neg = jnp.exp(-lin)
        e_tot = jnp.exp(tot)
        kh = kdir * e_neg
        bh = b * e_neg
        return dict(r=r, v=v, x1=x1, ksum=kdir_both[0] + kdir_both[1], rt=r * jnp.exp(lin),
                    at=kk * jnp.exp(lin - logw), kh=kh, bh=bh, kc=kh * e_tot, bc=bh * e_tot, e_tot=e_tot)

    chunk_of = (c, nc - 1 - c)
    pre = [prepare(d, chunk_of[d]) for d in range(2)]
    chains = [(d, h) for d in range(2) for h in range(H)]
    incl_m = (row >= col, row <= col)
    strict_m = (row > col, row < col)

    lhs2, s0, g2, w0s = {}, {}, {}, {}
    for d, h in chains:
        sl = slice(h * N, (h + 1) * N)
        q = pre[d]
        lhs2[d, h] = jnp.concatenate([q['at'][:, sl], q['rt'][:, sl]], axis=0)
        rhs2 = jnp.concatenate([q['kh'][:, sl], q['bh'][:, sl]], axis=0)
        g2[d, h] = _dot_nt(lhs2[d, h], rhs2)
        s0[d, h] = state_ref[d, h]
        w0s[d, h] = _dot_nt(lhs2[d, h], s0[d, h])
    av, x, u, a_rb = {}, {}, {}, {}
    for d, h in chains:
        sl = slice(h * N, (h + 1) * N)
        g = g2[d, h]
        a_ak = jnp.where(strict_m[d], g[0:C, 0:C], 0.0)
        a_rk = jnp.where(incl_m[d], g[C:2 * C, 0:C], 0.0)
        x[d, h] = jnp.where(strict_m[d], -g[0:C, C:2 * C], 0.0)
        a_rb[d, h] = jnp.where(incl_m[d], g[C:2 * C, C:2 * C], 0.0)
        av[d, h] = _dot(jnp.concatenate([a_ak, a_rk], axis=0), pre[d]['v'][:, sl])
    for d, h in chains:
        u[d, h] = w0s[d, h][0:C] + av[d, h][0:C]
    npow = int(math.log2(C))
    for i in range(npow):
        for d, h in chains:
            u[d, h] = u[d, h] + _dot(x[d, h], u[d, h])
        if i < npow - 1:
            for d, h in chains:
                x[d, h] = _dot(x[d, h], x[d, h])
    ys = {}
    for d, h in chains:
        sl = slice(h * N, (h + 1) * N)
        q = pre[d]
        ys[d, h] = w0s[d, h][C:2 * C] + av[d, h][C:2 * C] - _dot(a_rb[d, h], u[d, h])
        upd = _dot_tn(jnp.concatenate([q['v'][:, sl], u[d, h]], axis=0),
                      jnp.concatenate([q['kc'][:, sl], -q['bc'][:, sl]], axis=0))
        state_ref[d, h] = s0[d, h] * q['e_tot'][:, sl] + upd

    for d in range(2):
        q = pre[d]
        y = jnp.concatenate([ys[d, h] for h in range(H)], axis=1)
        base = pl.multiple_of(chunk_of[d] * C, C)

        @pl.when(c < nc // 2)
        def _():
            o_ref[0, pl.ds(base, C), :] = y

        @pl.when(c >= nc // 2)
        def _():
            yt = o_ref[0, pl.ds(base, C), :] + y
            inv_n = 1.0 / N
            mean = _dot_exact_rhs(yt, hsum) * inv_n
            yc = yt - mean
            var = _dot_exact_rhs(yc * yc, hsum) * inv_n
            yn = yc * lax.rsqrt(var + RWKV_GN_EPS) * gng_ref[...] + gnb_ref[...]
            coef = _dot_exact_rhs(q['r'] * rk_ref[...] * q['ksum'], hsum)
            gate = _dot(_sigmoid(q['x1']), gup_ref[...])
            o_ref[0, pl.ds(base, C), :] = (yn + coef * q['v']) * gate


def _rwkv_mixer(p_pad, mu_pad, w0, w_up, a0, a_up, g_up, k_k, k_a, r_k, gn_g, gn_b):
    B, S, PW = p_pad.shape
    W = RWKV_WIDTH
    C = RWKV_CHUNK
    wup = jnp.zeros((2, LANES, W), F32)
    for d in range(2):
        lo = GATE_LORA + d * DECAY_LORA
        wup = wup.at[d, lo:lo + DECAY_LORA].set(w_up[d])
    aup = jnp.zeros((2, LANES, W), F32)
    for d in range(2):
        aup = aup.at[d, d * ICLR_LORA:(d + 1) * ICLR_LORA].set(a_up[d])
    gup = jnp.zeros((LANES, W), F32).at[:GATE_LORA].set(g_up)
    head = jnp.arange(W) // HEAD_DIM
    hsum = (head[:, None] == head[None, :]).astype(BF16)
    row = lambda t: t.reshape(1, W)
    const2 = lambda shape: pl.BlockSpec(shape, lambda b, c: (0,) * len(shape))
    return pl.pallas_call(
        functools.partial(_rwkv_kernel, seq=S, chunk=C),
        grid=(B, S // C // 1),
        in_specs=[pl.BlockSpec((1, S, PW), lambda b, c: (b, 0, 0)),
                  const2((1, PW)), const2((2, LANES, W)), const2((2, LANES, W)), const2((LANES, W)),
                  const2((2, W)), const2((2, W)), const2((1, W)), const2((1, W)), const2((1, W)),
                  const2((1, W)), const2((1, W)), const2((W, W))],
        out_specs=pl.BlockSpec((1, S, W), lambda b, c: (b, 0, 0)),
        out_shape=jax.ShapeDtypeStruct((B, S, W), F32),
        scratch_shapes=[pltpu.VMEM((2, RWKV_HEADS, HEAD_DIM, HEAD_DIM), F32)],
        compiler_params=pltpu.CompilerParams(dimension_semantics=("arbitrary", "arbitrary"),
                                             vmem_limit_bytes=VMEM_LIMIT),
        name="rwkv7_scan",
    )(p_pad, mu_pad.reshape(1, PW), _bf(wup), _bf(aup), _bf(gup), w0, a0, row(k_k), row(k_a),
      r_k.reshape(1, W), row(gn_g), row(gn_b), hsum)


ATT_HEADS = 6
ATT_WIDTH = ATT_HEADS * HEAD_DIM
DILATED_BRANCHES = ((128, 1), (512, 4), (2048, 16))
ATT_BLOCK = 64
NEG_INF = -1e30
ATT_UNROLL = 8


def _rows(start, size, stride):
    return pl.ds(start, size) if stride == 1 else pl.ds(start, size, stride=stride)


def _att_kernel(slope_ref, q_ref, k_ref, v_ref, out_ref, o_s, m_s, l_s, *, seq):
    S, Q = seq, ATT_BLOCK
    left = lax.broadcasted_iota(jnp.int32, (1, LANES), 1) < HEAD_DIM
    qi = lax.broadcasted_iota(jnp.int32, (Q, 3 * Q), 0)
    kj = lax.broadcasted_iota(jnp.int32, (Q, 3 * Q), 1)
    dist = jnp.abs(kj - Q - qi).astype(F32)
    band = dist <= Q
    for bi, (window, dil) in enumerate(DILATED_BRANCHES):
        assert window // (2 * dil) == Q and S % (Q * dil) == 0
        nb = S // (Q * dil)
        bias = [jnp.where(band, (-dil * slope_ref[0, hh:hh + 1, 0:1]) * dist, NEG_INF) for hh in range(2)]

        def body(it, carry, dil=dil, nb=nb, bias=bias, bi=bi):
            blocks = []
            for uu in range(ATT_UNROLL):
                i = it * ATT_UNROLL + uu
                n = i % nb
                start = pl.multiple_of(i * Q, Q) if dil == 1 else i // nb + dil * Q * n
                s_prev = start - dil * Q * (n > 0).astype(jnp.int32)
                s_next = start + dil * Q * (n < nb - 1).astype(jnp.int32)
                if dil == 1:
                    s_prev, s_next = pl.multiple_of(s_prev, Q), pl.multiple_of(s_next, Q)
                q = q_ref[0, _rows(start, Q, dil), :] * (HEAD_DIM ** -0.5)
                kw = _bf(jnp.concatenate([k_ref[0, _rows(s, Q, dil), :] for s in (s_prev, start, s_next)], axis=0))
                vw = _bf(jnp.concatenate([v_ref[0, _rows(s, Q, dil), :] for s in (s_prev, start, s_next)], axis=0))
                pen_prev = jnp.where(n == 0, NEG_INF, 0.0)
                pen_next = jnp.where(n == nb - 1, NEG_INF, 0.0)
                edge = jnp.where(kj < Q, pen_prev, jnp.where(kj >= 2 * Q, pen_next, 0.0))
                blocks.append((start, q, kw, vw, edge))
            scores = [[_dot_nt(jnp.where(left if hh == 0 else jnp.logical_not(left), q, 0.0), kw) + bias[hh] + edge
                       for hh in range(2)] for (_, q, kw, _, edge) in blocks]
            mx = [[jnp.max(s, axis=-1, keepdims=True) for s in ss] for ss in scores]
            pr = [[jnp.exp(s - m) for s, m in zip(ss, ms)] for ss, ms in zip(scores, mx)]
            den = [[jnp.sum(p, axis=-1, keepdims=True) for p in ps] for ps in pr]
            pv = [[jnp.dot(_bf(p), blk[3], preferred_element_type=F32) for p in ps] for ps, blk in zip(pr, blocks)]
            for uu, blk in enumerate(blocks):
                rows = _rows(blk[0], Q, dil)
                o_s[bi, rows, :] = jnp.where(left, pv[uu][0], pv[uu][1])
                m_s[bi, rows, :] = jnp.where(left, mx[uu][0], mx[uu][1])
                l_s[bi, rows, :] = jnp.where(left, den[uu][0], den[uu][1])
            return carry

        lax.fori_loop(0, S // Q // ATT_UNROLL, body, 0)

    def merge(j, carry):
        rows = pl.ds(pl.multiple_of(j * Q, Q), Q)
        ms = [m_s[b, rows, :] for b in range(3)]
        top = jnp.maximum(jnp.maximum(ms[0], ms[1]), ms[2])
        ws = [jnp.exp(m - top) for m in ms]
        num = ws[0] * o_s[0, rows, :] + ws[1] * o_s[1, rows, :] + ws[2] * o_s[2, rows, :]
        den = ws[0] * l_s[0, rows, :] + ws[1] * l_s[1, rows, :] + ws[2] * l_s[2, rows, :]
        out_ref[0, rows, :] = num / den
        return carry

    lax.fori_loop(0, S // Q, merge, 0)


def _dilated_attention(q, k, v):
    B, S, W = q.shape
    pairs = W // LANES
    slopes = jnp.exp2(-8.0 * jnp.arange(1, ATT_HEADS + 1, dtype=F32) / ATT_HEADS).reshape(pairs, 2, 1)
    slopes = jnp.broadcast_to(jnp.pad(slopes, ((0, 0), (0, 6), (0, 0))), (pairs, 8, LANES))
    spec = pl.BlockSpec((1, S, LANES), lambda b, p: (b, 0, p))
    return pl.pallas_call(
        functools.partial(_att_kernel, seq=S),
        grid=(B, pairs),
        in_specs=[pl.BlockSpec((1, 8, LANES), lambda b, p: (p, 0, 0)), spec, spec, spec],
        out_specs=spec,
        out_shape=jax.ShapeDtypeStruct((B, S, W), F32),
        scratch_shapes=[pltpu.VMEM((3, S, LANES), F32)] * 3,
        compiler_params=pltpu.CompilerParams(dimension_semantics=("arbitrary", "arbitrary"),
                                             vmem_limit_bytes=VMEM_LIMIT),
        name="dilated_attention",
    )(slopes, q, k, v)


N_EXPERTS = 32
TOP_K = 4
SWIGLU_LIMIT = 7.0
SWIGLU_ALPHA = 1.702
MOE_ROWS = 512


CAST_ROWS = 128


def _expert_kernel(be_ref, nb_ref, x_ref, wgu_ref, bgu_ref, wd_ref, bd_ref, o_ref, wgu_s, wd_s):
    i = pl.program_id(0)
    F = wd_ref.shape[1]
    used = i < nb_ref[0]

    @pl.when(used & ((i == 0) | (be_ref[i] != be_ref[jnp.maximum(i - 1, 0)])))
    def _():
        def cast(src, dst):
            def step(j, carry):
                rows = pl.ds(pl.multiple_of(j * CAST_ROWS, CAST_ROWS), CAST_ROWS)
                dst[rows, :] = _bf(src[0, rows, :])
                return carry
            lax.fori_loop(0, src.shape[1] // CAST_ROWS, step, 0)
        cast(wgu_ref, wgu_s)
        cast(wd_ref, wd_s)

    @pl.when(used)
    def _():
        gu = jnp.dot(x_ref[...], wgu_s[...], preferred_element_type=F32) + bgu_ref[0]
        gate = jnp.minimum(gu[:, :F], SWIGLU_LIMIT)
        up = jnp.clip(gu[:, F:], -SWIGLU_LIMIT, SWIGLU_LIMIT)
        act = (up + 1.0) * gate * _sigmoid(SWIGLU_ALPHA * gate)
        o_ref[...] = jnp.dot(_bf(act), wd_s[...], preferred_element_type=F32) + bd_ref[0]

    @pl.when(jnp.logical_not(used))
    def _():
        o_ref[...] = jnp.zeros_like(o_ref)


def _expert_blocks(block_e, n_used, x_rows, w_gate_up, b_gate_up, w_down, b_down):
    n_rows, D = x_rows.shape
    E, _, F2 = w_gate_up.shape
    F = F2 // 2
    n_blocks = n_rows // MOE_ROWS
    return pl.pallas_call(
        _expert_kernel,
        grid_spec=pltpu.PrefetchScalarGridSpec(
            num_scalar_prefetch=2,
            grid=(n_blocks,),
            in_specs=[pl.BlockSpec((MOE_ROWS, D), lambda i, be, nb: (i, 0)),
                      pl.BlockSpec((1, D, F2), lambda i, be, nb: (be[i], 0, 0)),
                      pl.BlockSpec((1, 1, F2), lambda i, be, nb: (be[i], 0, 0)),
                      pl.BlockSpec((1, F, D), lambda i, be, nb: (be[i], 0, 0)),
                      pl.BlockSpec((1, 1, D), lambda i, be, nb: (be[i], 0, 0))],
            out_specs=pl.BlockSpec((MOE_ROWS, D), lambda i, be, nb: (i, 0)),
            scratch_shapes=[pltpu.VMEM((D, F2), BF16), pltpu.VMEM((F, D), BF16)]),
        out_shape=jax.ShapeDtypeStruct((n_rows, D), F32),
        compiler_params=pltpu.CompilerParams(dimension_semantics=("arbitrary",),
                                             vmem_limit_bytes=VMEM_LIMIT),
        name="moe_experts",
    )(block_e, n_used, x_rows, w_gate_up, b_gate_up.reshape(E, 1, F2), w_down, b_down.reshape(E, 1, D))


def _route(top_e, n_tokens):
    T = n_tokens
    n_rows = T * TOP_K + N_EXPERTS * MOE_ROWS
    n_blocks = n_rows // MOE_ROWS
    experts = jnp.arange(N_EXPERTS, dtype=jnp.int32)
    onehot = jnp.sum((top_e[:, :, None] == experts).astype(jnp.int32), axis=1)
    incl = jnp.cumsum(onehot, axis=0)
    counts = incl[-1]
    padded = (counts + MOE_ROWS - 1) // MOE_ROWS * MOE_ROWS
    starts = jnp.cumsum(counts) - counts
    pends = jnp.cumsum(padded)
    pstarts = pends - padded
    rank = jnp.take_along_axis(incl - onehot, top_e, axis=1)
    dest = pstarts[top_e] + rank
    order = jnp.argsort(top_e.reshape(-1), stable=True).astype(jnp.int32)
    block_start = jnp.arange(n_blocks, dtype=jnp.int32) * MOE_ROWS
    block_e = jnp.minimum(jnp.sum((pends[None, :] <= block_start[:, None]).astype(jnp.int32), axis=1),
                          N_EXPERTS - 1)
    shift = jnp.repeat((starts - pstarts)[block_e], MOE_ROWS)
    src = jnp.clip(jnp.arange(n_rows, dtype=jnp.int32) + shift, 0, T * TOP_K - 1)
    row_token = jnp.take(order, src, mode='clip') // TOP_K
    n_used = (pends[-1:] // MOE_ROWS).astype(jnp.int32)
    return dest.astype(jnp.int32), row_token.astype(jnp.int32), block_e.astype(jnp.int32), n_used


LN_EPS = 1e-5
MEM_HEADS = 4
MEM_WIDTH = MEM_HEADS * HEAD_DIM
RWKV_IN = 3 * RWKV_WIDTH + GATE_LORA + 2 * DECAY_LORA + 2 * ICLR_LORA
PROJ_ROWS = 512


def _ln(x, g, b):
    mu = jnp.mean(x, -1, keepdims=True)
    xc = x - mu
    var = jnp.mean(xc * xc, -1, keepdims=True)
    return xc * lax.rsqrt(var + LN_EPS) * g + b


def _in_proj_kernel(x_ref, g_ref, b_ref, w_ref, rw_ref, q_ref, k_ref, v_ref, qm_ref):
    hb = _bf(_ln(x_ref[...], g_ref[...], b_ref[...]))
    o = 0
    for ref in (rw_ref, q_ref, k_ref, v_ref, qm_ref):
        n = ref.shape[-1]
        ref[...] = jnp.dot(hb, w_ref[:, o:o + n], preferred_element_type=F32)
        o += n


def _in_proj(xt, ln_g, ln_b, w_pad):
    T, D = xt.shape
    widths = (RWKV_PAD_IN, ATT_WIDTH, ATT_WIDTH, ATT_WIDTH, MEM_WIDTH)
    assert sum(widths) == w_pad.shape[1]
    rows = lambda n: pl.BlockSpec((PROJ_ROWS, n), lambda i: (i, 0))
    full = lambda a: pl.BlockSpec(a.shape, lambda i: (0,) * a.ndim)
    args = (xt, ln_g.reshape(1, D), ln_b.reshape(1, D), w_pad)
    return pl.pallas_call(
        _in_proj_kernel,
        grid=(T // PROJ_ROWS,),
        in_specs=[rows(D)] + [full(a) for a in args[1:]],
        out_specs=[rows(n) for n in widths],
        out_shape=[jax.ShapeDtypeStruct((T, n), F32) for n in widths],
        compiler_params=pltpu.CompilerParams(dimension_semantics=("arbitrary",), vmem_limit_bytes=VMEM_LIMIT),
        name="in_proj",
    )(*args)


def _mem_att_kernel(q_ref, mem_ref, wkv_ref, o_ref, kv_s):
    W = MEM_WIDTH

    @pl.when(pl.program_id(1) == 0)
    def _():
        kv_s[...] = _bf(jnp.dot(_bf(mem_ref[0]), wkv_ref[...], preferred_element_type=F32))

    q = q_ref[0] * (HEAD_DIM ** -0.5)
    lane = lax.broadcasted_iota(jnp.int32, (1, W), 1)
    kmat = kv_s[:, 0:W]
    vmat = kv_s[:, W:2 * W]
    out = jnp.zeros(q.shape, F32)
    for h in range(MEM_HEADS):
        sel = (lane >= h * HEAD_DIM) & (lane < (h + 1) * HEAD_DIM)
        s = lax.dot_general(_bf(jnp.where(sel, q, 0.0)), kmat, (((1,), (1,)), ((), ())),
                            preferred_element_type=F32)
        p = jnp.exp(s - jnp.max(s, axis=-1, keepdims=True))
        pv = jnp.dot(_bf(p), vmat, preferred_element_type=F32)
        out = jnp.where(sel, pv / jnp.sum(p, axis=-1, keepdims=True), out)
    o_ref[0] = out


def _mem_attention(q_m, mem, w_kv):
    B, S, W = q_m.shape
    _, M, D = mem.shape
    return pl.pallas_call(
        _mem_att_kernel,
        grid=(B, S // PROJ_ROWS),
        in_specs=[pl.BlockSpec((1, PROJ_ROWS, W), lambda b, i: (b, i, 0)),
                  pl.BlockSpec((1, M, D), lambda b, i: (b, 0, 0)),
                  pl.BlockSpec((D, 2 * W), lambda b, i: (0, 0))],
        out_specs=pl.BlockSpec((1, PROJ_ROWS, W), lambda b, i: (b, i, 0)),
        out_shape=jax.ShapeDtypeStruct((B, S, W), F32),
        scratch_shapes=[pltpu.VMEM((M, 2 * W), BF16)],
        compiler_params=pltpu.CompilerParams(dimension_semantics=("arbitrary", "arbitrary"),
                                             vmem_limit_bytes=VMEM_LIMIT),
        name="memory_attention",
    )(q_m, mem, _bf(w_kv))


def _out_proj_kernel(x_ref, yr_ref, ya_ref, ym_ref, g0_ref, b0_ref, w_ref, g1_ref, b1_ref, wrh_ref, wrl_ref,
                     br_ref, h1_ref, h1b_ref, gate_ref, expert_ref, *, alpha):
    h0 = _ln(x_ref[...], g0_ref[...], b0_ref[...])
    o = 0
    mix = None
    for ref in (yr_ref, ya_ref, ym_ref):
        n = ref.shape[-1]
        part = jnp.dot(_bf(ref[...]), w_ref[o:o + n, :], preferred_element_type=F32)
        mix = part if mix is None else mix + part
        o += n
    h1 = _ln(alpha * h0 + mix, g1_ref[...], b1_ref[...])
    h1_ref[...] = h1
    hi = _bf(h1)
    h1b_ref[...] = hi
    lo = _bf(h1 - hi.astype(F32))
    logits = (jnp.dot(hi, wrh_ref[...], preferred_element_type=F32)
              + jnp.dot(lo, wrh_ref[...], preferred_element_type=F32)
              + jnp.dot(hi, wrl_ref[...], preferred_element_type=F32)) + br_ref[...]
    lane = lax.broadcasted_iota(jnp.int32, logits.shape, 1).astype(F32)
    vals = logits
    top_v, top_i = [], []
    for _ in range(TOP_K):
        m = jnp.max(vals, axis=-1, keepdims=True)
        idx = jnp.min(jnp.where(vals == m, lane, float(LANES)), axis=-1, keepdims=True)
        top_v.append(m)
        top_i.append(idx)
        vals = jnp.where(lane == idx, -jnp.inf, vals)
    ex = [jnp.exp(v - top_v[0]) for v in top_v]
    den = ex[0] + ex[1] + ex[2] + ex[3]
    gates = jnp.zeros(logits.shape, F32)
    experts = jnp.zeros(logits.shape, F32)
    for k in range(TOP_K):
        gates = jnp.where(lane == k, ex[k] / den, gates)
        experts = jnp.where(lane == k, top_i[k], experts)
    gate_ref[...] = gates
    expert_ref[...] = experts.astype(jnp.int32)


def _out_proj(xt, y_rwkv, y_att, y_mem, ln0_g, ln0_b, w_out, ln1_g, ln1_b, w_router, b_router, alpha):
    T, D = xt.shape
    E = w_router.shape[1]
    wr = jnp.pad(w_router, ((0, 0), (0, LANES - E)))
    wr_hi = _bf(wr)
    wr_lo = _bf(wr - wr_hi.astype(F32))
    br = jnp.pad(b_router, (0, LANES - E), constant_values=NEG_INF).reshape(1, LANES)
    rows = lambda n: pl.BlockSpec((PROJ_ROWS, n), lambda i: (i, 0))
    full = lambda a: pl.BlockSpec(a.shape, lambda i: (0,) * a.ndim)
    vec = lambda t: t.reshape(1, D)
    consts = (vec(ln0_g), vec(ln0_b), _bf(w_out), vec(ln1_g), vec(ln1_b), wr_hi, wr_lo, br)
    return pl.pallas_call(
        functools.partial(_out_proj_kernel, alpha=alpha),
        grid=(T // PROJ_ROWS,),
        in_specs=[rows(D), rows(y_rwkv.shape[1]), rows(y_att.shape[1]), rows(y_mem.shape[1])]
                 + [full(a) for a in consts],
        out_specs=[rows(D), rows(D), rows(LANES), rows(LANES)],
        out_shape=[jax.ShapeDtypeStruct((T, D), F32), jax.ShapeDtypeStruct((T, D), BF16),
                   jax.ShapeDtypeStruct((T, LANES), F32), jax.ShapeDtypeStruct((T, LANES), jnp.int32)],
        compiler_params=pltpu.CompilerParams(dimension_semantics=("arbitrary",), vmem_limit_bytes=VMEM_LIMIT),
        name="out_proj_router",
    )(xt, y_rwkv, y_att, y_mem, *consts)


def _combine_kernel(h1_ref, y_ref, gate_ref, g_ref, b_ref, o_ref, *, alpha):
    acc = alpha * h1_ref[...]
    gates = gate_ref[...]
    for k in range(TOP_K):
        acc = acc + gates[:, k:k + 1] * y_ref[k]
    o_ref[...] = _ln(acc, g_ref[...], b_ref[...])


def _combine(h1, y_top, gates, ln_g, ln_b, alpha):
    T, D = h1.shape
    rows = lambda n: pl.BlockSpec((PROJ_ROWS, n), lambda i: (i, 0))
    full = lambda a: pl.BlockSpec(a.shape, lambda i: (0,) * a.ndim)
    consts = (ln_g.reshape(1, D), ln_b.reshape(1, D))
    return pl.pallas_call(
        functools.partial(_combine_kernel, alpha=alpha),
        grid=(T // PROJ_ROWS,),
        in_specs=[rows(D), pl.BlockSpec((TOP_K, PROJ_ROWS, D), lambda i: (0, i, 0)), rows(LANES)]
                 + [full(a) for a in consts],
        out_specs=rows(D),
        out_shape=jax.ShapeDtypeStruct((T, D), F32),
        compiler_params=pltpu.CompilerParams(dimension_semantics=("arbitrary",), vmem_limit_bytes=VMEM_LIMIT),
        name="combine_ln",
    )(h1, y_top, gates, *consts)


def kernel(x, mem, ln_in_g, ln_in_b, w_in, mu_shift, w0, w_up, a0, a_up, g_up, k_k, k_a, r_k, gn_g, gn_b, w_mem_kv, w_out, ln1_g, ln1_b, w_router, b_router, w_gate_up, b_gate_up, w_down, b_down, ln2_g, ln2_b):
    B, S, D = x.shape
    T = B * S
    depth = w_in.shape[0]
    assert depth == 1, "the layer norm feeding a layer is fused into its projection kernels"
    alpha = (2 * depth) ** 0.25
    l = 0
    xt = x.reshape(T, D)
    pad = RWKV_PAD_IN - RWKV_IN
    w_pad = _bf(jnp.concatenate([w_in[l][:, :RWKV_IN], jnp.zeros((D, pad), F32), w_in[l][:, RWKV_IN:]], axis=1))
    mu_pad = jnp.pad(mu_shift[l], (0, pad))
    p_rwkv, q_a, k_a_, v_a, q_m = _in_proj(xt, ln_in_g, ln_in_b, w_pad)
    seq = lambda t: t.reshape(B, S, t.shape[-1])
    y_rwkv = _rwkv_mixer(seq(p_rwkv), mu_pad, w0[l], w_up[l], a0[l], a_up[l], g_up[l], k_k[l], k_a[l], r_k[l],
                         gn_g[l], gn_b[l])
    y_att = _dilated_attention(seq(q_a), seq(k_a_), seq(v_a))
    y_mem = _mem_attention(seq(q_m), mem, w_mem_kv[l])
    flat = lambda t: t.reshape(T, t.shape[-1])
    h1, h1b, gates, experts = _out_proj(xt, flat(y_rwkv), flat(y_att), flat(y_mem), ln_in_g, ln_in_b, w_out[l],
                                        ln1_g[l], ln1_b[l], w_router[l], b_router[l], alpha)
    dest, row_token, block_e, n_used = _route(experts[:, :TOP_K], T)
    x_rows = jnp.take(h1b, row_token, axis=0, mode='clip')
    y_rows = _expert_blocks(block_e, n_used, x_rows, w_gate_up[l], b_gate_up[l], w_down[l], b_down[l])
    y_top = jnp.take(y_rows, dest.T.reshape(-1), axis=0, mode='clip').reshape(TOP_K, T, D)
    out = _combine(h1, y_top, gates, ln2_g[l], ln2_b[l], alpha)
    return out.reshape(B, S, D)
```

```python
import functools
import math

import jax
import jax.numpy as jnp
from jax import lax
from jax.experimental import pallas as pl
from jax.experimental.pallas import tpu as pltpu

F32 = jnp.float32
BF16 = jnp.bfloat16

HEAD_DIM = 64
RWKV_HEADS = 6
RWKV_WIDTH = RWKV_HEADS * HEAD_DIM
GATE_LORA = 64
DECAY_LORA = 32
ICLR_LORA = 32
LANES = 128
RWKV_CHUNK = 64
RWKV_PAD_IN = 3 * RWKV_WIDTH + 2 * LANES
RWKV_GN_EPS = 64e-5
DECAY_SCALE = math.exp(-0.5)
VMEM_LIMIT = 56 * 1024 * 1024


def _bf(x):
    return x.astype(BF16)


def _dot(a, b):
    return jnp.dot(_bf(a), _bf(b), preferred_element_type=F32)


def _dot_nt(a, b):
    return lax.dot_general(_bf(a), _bf(b), (((1,), (1,)), ((), ())), preferred_element_type=F32)


def _dot_tn(a, b):
    return lax.dot_general(_bf(a), _bf(b), (((0,), (0,)), ((), ())), preferred_element_type=F32)


def _split3(x):
    hi = _bf(x)
    r1 = x - hi.astype(F32)
    mid = _bf(r1)
    lo = _bf(r1 - mid.astype(F32))
    return hi, mid, lo


def _dot_exact_lhs(a_bf, x, terms=3):
    parts = _split3(x)[:terms]
    acc = jnp.dot(a_bf, parts[0], preferred_element_type=F32)
    for p in parts[1:]:
        acc = acc + jnp.dot(a_bf, p, preferred_element_type=F32)
    return acc


def _dot_exact_rhs(x, b_bf, terms=2):
    parts = _split3(x)[:terms]
    acc = jnp.dot(parts[0], b_bf, preferred_element_type=F32)
    for p in parts[1:]:
        acc = acc + jnp.dot(p, b_bf, preferred_element_type=F32)
    return acc


def _sigmoid(x):
    return 1.0 / (1.0 + jnp.exp(-x))


def _rwkv_kernel(p_ref, mu_ref, wup_ref, aup_ref, gup_ref, w0_ref, a0_ref, kk_ref, ka_ref, rk_ref,
                 gng_ref, gnb_ref, hsum_ref, o_ref, state_ref, *, seq, chunk):
    C, N, H, W = chunk, HEAD_DIM, RWKV_HEADS, RWKV_WIDTH
    nc = seq // C
    c = pl.program_id(1)

    @pl.when(c == 0)
    def _():
        state_ref[...] = jnp.zeros_like(state_ref)

    hsum = hsum_ref[...]
    row = lax.broadcasted_iota(jnp.int32, (C, C), 0)
    col = lax.broadcasted_iota(jnp.int32, (C, C), 1)
    rowc = lax.broadcasted_iota(jnp.int32, (C, 1), 0)

    def shifted(ci):
        base = pl.multiple_of(ci * C, C)
        cur = p_ref[0, pl.ds(base, C), :]
        pb = p_ref[0, pl.ds(pl.multiple_of(jnp.maximum(base - 8, 0), 8), 8), :]
        nb = p_ref[0, pl.ds(pl.multiple_of(jnp.minimum(base + C, seq - 8), 8), 8), :]
        prev_row = jnp.where(ci > 0, pb[7:8, :], 0.0)
        next_row = jnp.where(ci < nc - 1, nb[0:1, :], 0.0)
        prev = jnp.where(rowc == 0, prev_row, pltpu.roll(cur, 1, 0))
        nxt = jnp.where(rowc == C - 1, next_row, pltpu.roll(cur, C - 1, 0))
        return cur + mu_ref[...] * (0.5 * (prev + nxt) - cur)

    def prepare(d, ci):
        ps = shifted(ci)
        r = ps[:, 0:W]
        k = ps[:, W:2 * W]
        v = ps[:, 2 * W:3 * W]
        x1 = ps[:, 3 * W:3 * W + LANES]
        x2 = ps[:, 3 * W + LANES:3 * W + 2 * LANES]
        a_both = [_sigmoid(a0_ref[e:e + 1, :] + _dot(x2, aup_ref[e])) for e in range(2)]
        w_logit = w0_ref[d:d + 1, :] + _dot(jnp.tanh(x1), wup_ref[d])
        logw = -DECAY_SCALE * _sigmoid(w_logit)
        kkr = k * kk_ref[...]
        ss = _dot_exact_rhs(kkr * kkr, hsum)
        kk = kkr / jnp.maximum(jnp.sqrt(ss), 1e-12)
        kdir_both = [k * (1.0 + (a - 1.0) * ka_ref[...]) for a in a_both]
        kdir = kdir_both[d]
        b = kk * a_both[d]
        incl = (row >= col) if d == 0 else (row <= col)
        tri = jnp.where(incl, 1.0, 0.0).astype(BF16)
        lin = _dot_exact_lhs(tri, logw)
        tot = jnp.sum(logw, axis=0, keepdims=True)
        e_neg = jnp.exp(-lin)
        e_tot = jnp.exp(tot)
        kh = kdir * e_neg
        bh = b * e_neg
        return dict(r=r, v=v, x1=x1, ksum=kdir_both[0] + kdir_both[1], rt=r * jnp.exp(lin),
                    at=kk * jnp.exp(lin - logw), kh=kh, bh=bh, kc=kh * e_tot, bc=bh * e_tot, e_tot=e_tot)

    chunk_of = (c, nc - 1 - c)
    pre = [prepare(d, chunk_of[d]) for d in range(2)]
    chains = [(d, h) for d in range(2) for h in range(H)]
    assert 2 * N == LANES and C == N
    r2 = lax.broadcasted_iota(jnp.int32, (2 * C, LANES), 0)
    l2 = lax.broadcasted_iota(jnp.int32, (2 * C, LANES), 1)
    t_row = jnp.where(r2 >= C, r2 - C, r2)
    t_col = jnp.where(l2 >= N, l2 - N, l2)
    before = (t_row > t_col, t_row < t_col)
    diag_r = jnp.logical_and(r2 >= C, t_row == t_col)
    tile_mask = [jnp.logical_or(before[d], diag_r) for d in range(2)]
    left = lax.broadcasted_iota(jnp.int32, (C, LANES), 1) < N
    zeros_cn = jnp.zeros((C, N), F32)
    zeros_cl = jnp.zeros((C, LANES), F32)

    s0, tile0, ws = {}, {}, {}
    for d, h in chains:
        sl = slice(h * N, (h + 1) * N)
        q = pre[d]
        lhs2 = jnp.concatenate([q['at'][:, sl], q['rt'][:, sl]], axis=0)
        s0[d, h] = state_ref[d, h]
        rhs3 = jnp.concatenate([q['kh'][:, sl], q['bh'][:, sl], s0[d, h]], axis=0)
        g = _dot_nt(lhs2, rhs3)
        tile0[d, h] = jnp.where(tile_mask[d], g[:, 0:LANES], 0.0)
        ws[d, h] = g[:, LANES:LANES + N]
    av = {}
    for d, h in chains:
        sl = slice(h * N, (h + 1) * N)
        lhs = jnp.where(l2 < N, tile0[d, h], 0.0)
        av[d, h] = _dot(lhs, jnp.concatenate([pre[d]['v'][:, sl], zeros_cn], axis=0))
    z = {}
    for d, h in chains:
        u0 = jnp.concatenate([ws[d, h][0:C] + av[d, h][0:C], zeros_cn], axis=1)
        z[d, h] = jnp.where(left, u0, -tile0[d, h][0:C])
    for _ in range(int(math.log2(C))):
        for d, h in chains:
            prod = _dot(jnp.where(left, 0.0, z[d, h]), jnp.concatenate([zeros_cl, z[d, h]], axis=0))
            z[d, h] = jnp.where(left, z[d, h] + prod, prod)
    ys = {}
    for d, h in chains:
        sl = slice(h * N, (h + 1) * N)
        q = pre[d]
        u = z[d, h][:, 0:N]
        rb = jnp.where(left, 0.0, tile0[d, h][C:2 * C])
        ys[d, h] = ws[d, h][C:2 * C] + av[d, h][C:2 * C] - _dot(rb, jnp.concatenate([zeros_cn, u], axis=0))
        upd = _dot_tn(jnp.concatenate([q['v'][:, sl], u], axis=0),
                      jnp.concatenate([q['kc'][:, sl], -q['bc'][:, sl]], axis=0))
        state_ref[d, h] = s0[d, h] * q['e_tot'][:, sl] + upd

    y = [jnp.concatenate([ys[d, h] for h in range(H)], axis=1) for d in range(2)]
    base = [pl.multiple_of(chunk_of[d] * C, C) for d in range(2)]

    @pl.when(c < nc // 2)
    def _():
        for d in range(2):
            o_ref[0, pl.ds(base[d], C), :] = y[d]

    @pl.when(c >= nc // 2)
    def _():
        inv_n = 1.0 / N
        yt = [o_ref[0, pl.ds(base[d], C), :] + y[d] for d in range(2)]
        mean = [_dot_exact_rhs(t, hsum) * inv_n for t in yt]
        yc = [t - m for t, m in zip(yt, mean)]
        var = [_dot_exact_rhs(t * t, hsum) * inv_n for t in yc]
        for d in range(2):
            q = pre[d]
            yn = yc[d] * lax.rsqrt(var[d] + RWKV_GN_EPS) * gng_ref[...] + gnb_ref[...]
            coef = _dot_exact_rhs(q['r'] * rk_ref[...] * q['ksum'], hsum)
            gate = _dot(_sigmoid(q['x1']), gup_ref[...])
            o_ref[0, pl.ds(base[d], C), :] = (yn + coef * q['v']) * gate


def _rwkv_mixer(p_pad, mu_pad, w0, w_up, a0, a_up, g_up, k_k, k_a, r_k, gn_g, gn_b):
    B, S, PW = p_pad.shape
    W = RWKV_WIDTH
    C = RWKV_CHUNK
    wup = jnp.zeros((2, LANES, W), F32)
    for d in range(2):
        lo = GATE_LORA + d * DECAY_LORA
        wup = wup.at[d, lo:lo + DECAY_LORA].set(w_up[d])
    aup = jnp.zeros((2, LANES, W), F32)
    for d in range(2):
        aup = aup.at[d, d * ICLR_LORA:(d + 1) * ICLR_LORA].set(a_up[d])
    gup = jnp.zeros((LANES, W), F32).at[:GATE_LORA].set(g_up)
    head = jnp.arange(W) // HEAD_DIM
    hsum = (head[:, None] == head[None, :]).astype(BF16)
    row = lambda t: t.reshape(1, W)
    const2 = lambda shape: pl.BlockSpec(shape, lambda b, c: (0,) * len(shape))
    return pl.pallas_call(
        functools.partial(_rwkv_kernel, seq=S, chunk=C),
        grid=(B, S // C // 1),
        in_specs=[pl.BlockSpec((1, S, PW), lambda b, c: (b, 0, 0)),
                  const2((1, PW)), const2((2, LANES, W)), const2((2, LANES, W)), const2((LANES, W)),
                  const2((2, W)), const2((2, W)), const2((1, W)), const2((1, W)), const2((1, W)),
                  const2((1, W)), const2((1, W)), const2((W, W))],
        out_specs=pl.BlockSpec((1, S, W), lambda b, c: (b, 0, 0)),
        out_shape=jax.ShapeDtypeStruct((B, S, W), F32),
        scratch_shapes=[pltpu.VMEM((2, RWKV_HEADS, HEAD_DIM, HEAD_DIM), F32)],
        compiler_params=pltpu.CompilerParams(dimension_semantics=("arbitrary", "arbitrary"),
                                             vmem_limit_bytes=VMEM_LIMIT),
        name="rwkv7_scan",
    )(p_pad, mu_pad.reshape(1, PW), _bf(wup), _bf(aup), _bf(gup), w0, a0, row(k_k), row(k_a),
      r_k.reshape(1, W), row(gn_g), row(gn_b), hsum)


ATT_HEADS = 6
ATT_WIDTH = ATT_HEADS * HEAD_DIM
DILATED_BRANCHES = ((128, 1), (512, 4), (2048, 16))
ATT_BLOCK = 64
NEG_INF = -1e30
ATT_UNROLL = 8


def _rows(start, size, stride):
    return pl.ds(start, size) if stride == 1 else pl.ds(start, size, stride=stride)


def _att_kernel(slope_ref, q_ref, k_ref, v_ref, out_ref, o_s, m_s, l_s, *, seq):
    S, Q = seq, ATT_BLOCK
    left = lax.broadcasted_iota(jnp.int32, (1, LANES), 1) < HEAD_DIM
    qi = lax.broadcasted_iota(jnp.int32, (Q, 3 * Q), 0)
    kj = lax.broadcasted_iota(jnp.int32, (Q, 3 * Q), 1)
    dist = jnp.abs(kj - Q - qi).astype(F32)
    band = dist <= Q
    for bi, (window, dil) in enumerate(DILATED_BRANCHES):
        assert window // (2 * dil) == Q and S % (Q * dil) == 0
        nb = S // (Q * dil)
        bias = [jnp.where(band, (-dil * slope_ref[0, hh:hh + 1, 0:1]) * dist, NEG_INF) for hh in range(2)]

        def body(it, carry, dil=dil, nb=nb, bias=bias, bi=bi):
            blocks = []
            for uu in range(ATT_UNROLL):
                i = it * ATT_UNROLL + uu
                n = i % nb
                start = pl.multiple_of(i * Q, Q) if dil == 1 else i // nb + dil * Q * n
                s_prev = start - jnp.where(n > 0, dil * Q, 0)
                s_next = start + jnp.where(n < nb - 1, dil * Q, 0)
                if dil == 1:
                    s_prev, s_next = pl.multiple_of(s_prev, Q), pl.multiple_of(s_next, Q)
                q = q_ref[0, _rows(start, Q, dil), :] * (HEAD_DIM ** -0.5)
                kw = _bf(jnp.concatenate([k_ref[0, _rows(s, Q, dil), :] for s in (s_prev, start, s_next)], axis=0))
                vw = _bf(jnp.concatenate([v_ref[0, _rows(s, Q, dil), :] for s in (s_prev, start, s_next)], axis=0))
                pen_prev = jnp.where(n == 0, NEG_INF, 0.0)
                pen_next = jnp.where(n == nb - 1, NEG_INF, 0.0)
                edge = jnp.where(kj < Q, pen_prev, jnp.where(kj >= 2 * Q, pen_next, 0.0))
                blocks.append((start, q, kw, vw, edge))
            scores = [[_dot_nt(jnp.where(left if hh == 0 else jnp.logical_not(left), q, 0.0), kw) + bias[hh] + edge
                       for hh in range(2)] for (_, q, kw, _, edge) in blocks]
            mx = [[jnp.max(s, axis=-1, keepdims=True) for s in ss] for ss in scores]
            pr = [[jnp.exp(s - m) for s, m in zip(ss, ms)] for ss, ms in zip(scores, mx)]
            den = [[jnp.sum(p, axis=-1, keepdims=True) for p in ps] for ps in pr]
            pv = [[jnp.dot(_bf(p), blk[3], preferred_element_type=F32) for p in ps] for ps, blk in zip(pr, blocks)]
            for uu, blk in enumerate(blocks):
                rows = _rows(blk[0], Q, dil)
                o_s[bi, rows, :] = jnp.where(left, pv[uu][0], pv[uu][1])
                m_s[bi, rows, :] = jnp.where(left, mx[uu][0], mx[uu][1])
                l_s[bi, rows, :] = jnp.where(left, den[uu][0], den[uu][1])
            return carry

        lax.fori_loop(0, S // Q // ATT_UNROLL, body, 0)

    def merge(j, carry):
        rows = pl.ds(pl.multiple_of(j * Q, Q), Q)
        ms = [m_s[b, rows, :] for b in range(3)]
        top = jnp.maximum(jnp.maximum(ms[0], ms[1]), ms[2])
        ws = [jnp.exp(m - top) for m in ms]
        num = ws[0] * o_s[0, rows, :] + ws[1] * o_s[1, rows, :] + ws[2] * o_s[2, rows, :]
        den = ws[0] * l_s[0, rows, :] + ws[1] * l_s[1, rows, :] + ws[2] * l_s[2, rows, :]
        out_ref[0, rows, :] = num / den
        return carry

    lax.fori_loop(0, S // Q, merge, 0)


def _dilated_attention(q, k, v):
    B, S, W = q.shape
    pairs = W // LANES
    slopes = jnp.exp2(-8.0 * jnp.arange(1, ATT_HEADS + 1, dtype=F32) / ATT_HEADS).reshape(pairs, 2, 1)
    slopes = jnp.broadcast_to(jnp.pad(slopes, ((0, 0), (0, 6), (0, 0))), (pairs, 8, LANES))
    spec = pl.BlockSpec((1, S, LANES), lambda b, p: (b, 0, p))
    return pl.pallas_call(
        functools.partial(_att_kernel, seq=S),
        grid=(B, pairs),
        in_specs=[pl.BlockSpec((1, 8, LANES), lambda b, p: (p, 0, 0)), spec, spec, spec],
        out_specs=spec,
        out_shape=jax.ShapeDtypeStruct((B, S, W), F32),
        scratch_shapes=[pltpu.VMEM((3, S, LANES), F32)] * 3,
        compiler_params=pltpu.CompilerParams(dimension_semantics=("arbitrary", "arbitrary"),
                                             vmem_limit_bytes=VMEM_LIMIT),
        name="dilated_attention",
    )(slopes, q, k, v)


N_EXPERTS = 32
TOP_K = 4
SWIGLU_LIMIT = 7.0
SWIGLU_ALPHA = 1.702
MOE_ROWS = 512


CAST_ROWS = 128


def _expert_kernel(be_ref, nb_ref, x_ref, wgu_ref, bgu_ref, wd_ref, bd_ref, o_ref, wgu_s, wd_s):
    i = pl.program_id(0)
    F = wd_ref.shape[1]
    used = i < nb_ref[0]

    @pl.when(used & ((i == 0) | (be_ref[i] != be_ref[jnp.maximum(i - 1, 0)])))
    def _():
        def cast(src, dst):
            def step(j, carry):
                rows = pl.ds(pl.multiple_of(j * CAST_ROWS, CAST_ROWS), CAST_ROWS)
                dst[rows, :] = _bf(src[0, rows, :])
                return carry
            lax.fori_loop(0, src.shape[1] // CAST_ROWS, step, 0)
        cast(wgu_ref, wgu_s)
        cast(wd_ref, wd_s)

    @pl.when(used)
    def _():
        gu = jnp.dot(x_ref[...], wgu_s[...], preferred_element_type=F32) + bgu_ref[0]
        gate = jnp.minimum(gu[:, :F], SWIGLU_LIMIT)
        up = jnp.clip(gu[:, F:], -SWIGLU_LIMIT, SWIGLU_LIMIT)
        act = (up + 1.0) * gate * _sigmoid(SWIGLU_ALPHA * gate)
        o_ref[...] = (jnp.dot(_bf(act), wd_s[...], preferred_element_type=F32) + bd_ref[0]).astype(o_ref.dtype)

    @pl.when(jnp.logical_not(used))
    def _():
        o_ref[...] = jnp.zeros_like(o_ref)


def _expert_blocks(block_e, n_used, x_rows, w_gate_up, b_gate_up, w_down, b_down):
    n_rows, D = x_rows.shape
    E, _, F2 = w_gate_up.shape
    F = F2 // 2
    n_blocks = n_rows // MOE_ROWS
    return pl.pallas_call(
        _expert_kernel,
        grid_spec=pltpu.PrefetchScalarGridSpec(
            num_scalar_prefetch=2,
            grid=(n_blocks,),
            in_specs=[pl.BlockSpec((MOE_ROWS, D), lambda i, be, nb: (i, 0)),
                      pl.BlockSpec((1, D, F2), lambda i, be, nb: (be[i], 0, 0)),
                      pl.BlockSpec((1, 1, F2), lambda i, be, nb: (be[i], 0, 0)),
                      pl.BlockSpec((1, F, D), lambda i, be, nb: (be[i], 0, 0)),
                      pl.BlockSpec((1, 1, D), lambda i, be, nb: (be[i], 0, 0))],
            out_specs=pl.BlockSpec((MOE_ROWS, D), lambda i, be, nb: (i, 0)),
            scratch_shapes=[pltpu.VMEM((D, F2), BF16), pltpu.VMEM((F, D), BF16)]),
        out_shape=jax.ShapeDtypeStruct((n_rows, D), BF16),
        compiler_params=pltpu.CompilerParams(dimension_semantics=("arbitrary",),
                                             vmem_limit_bytes=VMEM_LIMIT),
        name="moe_experts",
    )(block_e, n_used, x_rows, w_gate_up, b_gate_up.reshape(E, 1, F2), w_down, b_down.reshape(E, 1, D))


def _route(top_e, n_tokens):
    T = n_tokens
    n_rows = T * TOP_K + N_EXPERTS * MOE_ROWS
    n_blocks = n_rows // MOE_ROWS
    experts = jnp.arange(N_EXPERTS, dtype=jnp.int32)
    onehot = jnp.sum((top_e[:, :, None] == experts).astype(jnp.int32), axis=1)
    incl = jnp.cumsum(onehot, axis=0)
    counts = incl[-1]
    padded = (counts + MOE_ROWS - 1) // MOE_ROWS * MOE_ROWS
    starts = jnp.cumsum(counts) - counts
    pends = jnp.cumsum(padded)
    pstarts = pends - padded
    rank = jnp.take_along_axis(incl - onehot, top_e, axis=1)
    dest = pstarts[top_e] + rank
    order = jnp.argsort(top_e.reshape(-1), stable=True).astype(jnp.int32)
    block_start = jnp.arange(n_blocks, dtype=jnp.int32) * MOE_ROWS
    block_e = jnp.minimum(jnp.sum((pends[None, :] <= block_start[:, None]).astype(jnp.int32), axis=1),
                          N_EXPERTS - 1)
    shift = jnp.repeat((starts - pstarts)[block_e], MOE_ROWS)
    src = jnp.clip(jnp.arange(n_rows, dtype=jnp.int32) + shift, 0, T * TOP_K - 1)
    row_token = jnp.take(order, src, mode='clip') // TOP_K
    n_used = (pends[-1:] // MOE_ROWS).astype(jnp.int32)
    return dest.astype(jnp.int32), row_token.astype(jnp.int32), block_e.astype(jnp.int32), n_used


LN_EPS = 1e-5
MEM_HEADS = 4
MEM_WIDTH = MEM_HEADS * HEAD_DIM
RWKV_IN = 3 * RWKV_WIDTH + GATE_LORA + 2 * DECAY_LORA + 2 * ICLR_LORA
PROJ_ROWS = 512


def _ln(x, g, b):
    mu = jnp.mean(x, -1, keepdims=True)
    xc = x - mu
    var = jnp.mean(xc * xc, -1, keepdims=True)
    return xc * lax.rsqrt(var + LN_EPS) * g + b


def _in_proj_kernel(x_ref, g_ref, b_ref, w_ref, rw_ref, q_ref, k_ref, v_ref, qm_ref):
    hb = _bf(_ln(x_ref[...], g_ref[...], b_ref[...]))
    o = 0
    for ref in (rw_ref, q_ref, k_ref, v_ref, qm_ref):
        n = ref.shape[-1]
        ref[...] = jnp.dot(hb, w_ref[:, o:o + n], preferred_element_type=F32)
        o += n


def _in_proj(xt, ln_g, ln_b, w_pad):
    T, D = xt.shape
    widths = (RWKV_PAD_IN, ATT_WIDTH, ATT_WIDTH, ATT_WIDTH, MEM_WIDTH)
    assert sum(widths) == w_pad.shape[1]
    rows = lambda n: pl.BlockSpec((PROJ_ROWS, n), lambda i: (i, 0))
    full = lambda a: pl.BlockSpec(a.shape, lambda i: (0,) * a.ndim)
    args = (xt, ln_g.reshape(1, D), ln_b.reshape(1, D), w_pad)
    return pl.pallas_call(
        _in_proj_kernel,
        grid=(T // PROJ_ROWS,),
        in_specs=[rows(D)] + [full(a) for a in args[1:]],
        out_specs=[rows(n) for n in widths],
        out_shape=[jax.ShapeDtypeStruct((T, n), F32) for n in widths],
        compiler_params=pltpu.CompilerParams(dimension_semantics=("arbitrary",), vmem_limit_bytes=VMEM_LIMIT),
        name="in_proj",
    )(*args)


def _mem_att_kernel(q_ref, mem_ref, wkv_ref, o_ref, kv_s):
    W = MEM_WIDTH

    @pl.when(pl.program_id(1) == 0)
    def _():
        kv_s[...] = _bf(jnp.dot(_bf(mem_ref[0]), wkv_ref[...], preferred_element_type=F32))

    q = q_ref[0] * (HEAD_DIM ** -0.5)
    lane = lax.broadcasted_iota(jnp.int32, (1, W), 1)
    kmat = kv_s[:, 0:W]
    vmat = kv_s[:, W:2 * W]
    out = jnp.zeros(q.shape, F32)
    for h in range(MEM_HEADS):
        sel = (lane >= h * HEAD_DIM) & (lane < (h + 1) * HEAD_DIM)
        s = lax.dot_general(_bf(jnp.where(sel, q, 0.0)), kmat, (((1,), (1,)), ((), ())),
                            preferred_element_type=F32)
        p = jnp.exp(s - jnp.max(s, axis=-1, keepdims=True))
        pv = jnp.dot(_bf(p), vmat, preferred_element_type=F32)
        out = jnp.where(sel, pv / jnp.sum(p, axis=-1, keepdims=True), out)
    o_ref[0] = out


def _mem_attention(q_m, mem, w_kv):
    B, S, W = q_m.shape
    _, M, D = mem.shape
    return pl.pallas_call(
        _mem_att_kernel,
        grid=(B, S // PROJ_ROWS),
        in_specs=[pl.BlockSpec((1, PROJ_ROWS, W), lambda b, i: (b, i, 0)),
                  pl.BlockSpec((1, M, D), lambda b, i: (b, 0, 0)),
                  pl.BlockSpec((D, 2 * W), lambda b, i: (0, 0))],
        out_specs=pl.BlockSpec((1, PROJ_ROWS, W), lambda b, i: (b, i, 0)),
        out_shape=jax.ShapeDtypeStruct((B, S, W), F32),
        scratch_shapes=[pltpu.VMEM((M, 2 * W), BF16)],
        compiler_params=pltpu.CompilerParams(dimension_semantics=("arbitrary", "arbitrary"),
                                             vmem_limit_bytes=VMEM_LIMIT),
        name="memory_attention",
    )(q_m, mem, _bf(w_kv))


def _out_proj_kernel(x_ref, yr_ref, ya_ref, ym_ref, g0_ref, b0_ref, w_ref, g1_ref, b1_ref, wrh_ref, wrl_ref,
                     br_ref, h1_ref, h1b_ref, gate_ref, expert_ref, *, alpha):
    h0 = _ln(x_ref[...], g0_ref[...], b0_ref[...])
    o = 0
    mix = None
    for ref in (yr_ref, ya_ref, ym_ref):
        n = ref.shape[-1]
        part = jnp.dot(_bf(ref[...]), w_ref[o:o + n, :], preferred_element_type=F32)
        mix = part if mix is None else mix + part
        o += n
    h1 = _ln(alpha * h0 + mix, g1_ref[...], b1_ref[...])
    h1_ref[...] = h1
    hi = _bf(h1)
    h1b_ref[...] = hi
    lo = _bf(h1 - hi.astype(F32))
    logits = (jnp.dot(hi, wrh_ref[...], preferred_element_type=F32)
              + jnp.dot(lo, wrh_ref[...], preferred_element_type=F32)
              + jnp.dot(hi, wrl_ref[...], preferred_element_type=F32)) + br_ref[...]
    lane = lax.broadcasted_iota(jnp.int32, logits.shape, 1).astype(F32)
    vals = logits
    top_v, top_i = [], []
    for _ in range(TOP_K):
        m = jnp.max(vals, axis=-1, keepdims=True)
        idx = jnp.min(jnp.where(vals == m, lane, float(LANES)), axis=-1, keepdims=True)
        top_v.append(m)
        top_i.append(idx)
        vals = jnp.where(lane == idx, -jnp.inf, vals)
    ex = [jnp.exp(v - top_v[0]) for v in top_v]
    den = ex[0] + ex[1] + ex[2] + ex[3]
    gates = jnp.zeros(logits.shape, F32)
    experts = jnp.zeros(logits.shape, F32)
    for k in range(TOP_K):
        gates = jnp.where(lane == k, ex[k] / den, gates)
        experts = jnp.where(lane == k, top_i[k], experts)
    gate_ref[...] = gates
    expert_ref[...] = experts.astype(jnp.int32)


def _out_proj(xt, y_rwkv, y_att, y_mem, ln0_g, ln0_b, w_out, ln1_g, ln1_b, w_router, b_router, alpha):
    T, D = xt.shape
    E = w_router.shape[1]
    wr = jnp.pad(w_router, ((0, 0), (0, LANES - E)))
    wr_hi = _bf(wr)
    wr_lo = _bf(wr - wr_hi.astype(F32))
    br = jnp.pad(b_router, (0, LANES - E), constant_values=NEG_INF).reshape(1, LANES)
    rows = lambda n: pl.BlockSpec((PROJ_ROWS, n), lambda i: (i, 0))
    full = lambda a: pl.BlockSpec(a.shape, lambda i: (0,) * a.ndim)
    vec = lambda t: t.reshape(1, D)
    consts = (vec(ln0_g), vec(ln0_b), _bf(w_out), vec(ln1_g), vec(ln1_b), wr_hi, wr_lo, br)
    return pl.pallas_call(
        functools.partial(_out_proj_kernel, alpha=alpha),
        grid=(T // PROJ_ROWS,),
        in_specs=[rows(D), rows(y_rwkv.shape[1]), rows(y_att.shape[1]), rows(y_mem.shape[1])]
                 + [full(a) for a in consts],
        out_specs=[rows(D), rows(D), rows(LANES), rows(LANES)],
        out_shape=[jax.ShapeDtypeStruct((T, D), F32), jax.ShapeDtypeStruct((T, D), BF16),
                   jax.ShapeDtypeStruct((T, LANES), F32), jax.ShapeDtypeStruct((T, LANES), jnp.int32)],
        compiler_params=pltpu.CompilerParams(dimension_semantics=("arbitrary",), vmem_limit_bytes=VMEM_LIMIT),
        name="out_proj_router",
    )(xt, y_rwkv, y_att, y_mem, *consts)


def _combine_kernel(h1_ref, y_ref, gate_ref, g_ref, b_ref, o_ref, *, alpha):
    acc = alpha * h1_ref[...]
    gates = gate_ref[...]
    for k in range(TOP_K):
        acc = acc + gates[:, k:k + 1] * y_ref[k].astype(F32)
    o_ref[...] = _ln(acc, g_ref[...], b_ref[...])


def _combine(h1, y_top, gates, ln_g, ln_b, alpha):
    T, D = h1.shape
    rows = lambda n: pl.BlockSpec((PROJ_ROWS, n), lambda i: (i, 0))
    full = lambda a: pl.BlockSpec(a.shape, lambda i: (0,) * a.ndim)
    consts = (ln_g.reshape(1, D), ln_b.reshape(1, D))
    return pl.pallas_call(
        functools.partial(_combine_kernel, alpha=alpha),
        grid=(T // PROJ_ROWS,),
        in_specs=[rows(D), pl.BlockSpec((TOP_K, PROJ_ROWS, D), lambda i: (0, i, 0)), rows(LANES)]
                 + [full(a) for a in consts],
        out_specs=rows(D),
        out_shape=jax.ShapeDtypeStruct((T, D), F32),
        compiler_params=pltpu.CompilerParams(dimension_semantics=("arbitrary",), vmem_limit_bytes=VMEM_LIMIT),
        name="combine_ln",
    )(h1, y_top, gates, *consts)


def kernel(x, mem, ln_in_g, ln_in_b, w_in, mu_shift, w0, w_up, a0, a_up, g_up, k_k, k_a, r_k, gn_g, gn_b, w_mem_kv, w_out, ln1_g, ln1_b, w_router, b_router, w_gate_up, b_gate_up, w_down, b_down, ln2_g, ln2_b):
    B, S, D = x.shape
    T = B * S
    depth = w_in.shape[0]
    assert depth == 1, "the layer norm feeding a layer is fused into its projection kernels"
    alpha = (2 * depth) ** 0.25
    l = 0
    xt = x.reshape(T, D)
    pad = RWKV_PAD_IN - RWKV_IN
    w_pad = _bf(jnp.concatenate([w_in[l][:, :RWKV_IN], jnp.zeros((D, pad), F32), w_in[l][:, RWKV_IN:]], axis=1))
    mu_pad = jnp.pad(mu_shift[l], (0, pad))
    p_rwkv, q_a, k_a_, v_a, q_m = _in_proj(xt, ln_in_g, ln_in_b, w_pad)
    seq = lambda t: t.reshape(B, S, t.shape[-1])
    y_rwkv = _rwkv_mixer(seq(p_rwkv), mu_pad, w0[l], w_up[l], a0[l], a_up[l], g_up[l], k_k[l], k_a[l], r_k[l],
                         gn_g[l], gn_b[l])
    y_att = _dilated_attention(seq(q_a), seq(k_a_), seq(v_a))
    y_mem = _mem_attention(seq(q_m), mem, w_mem_kv[l])
    flat = lambda t: t.reshape(T, t.shape[-1])
    h1, h1b, gates, experts = _out_proj(xt, flat(y_rwkv), flat(y_att), flat(y_mem), ln_in_g, ln_in_b, w_out[l],
                                        ln1_g[l], ln1_b[l], w_router[l], b_router[l], alpha)
    dest, row_token, block_e, n_used = _route(experts[:, :TOP_K], T)
    x_rows = jnp.take(h1b, row_token, axis=0, mode='clip')
    y_rows = _expert_blocks(block_e, n_used, x_rows, w_gate_up[l], b_gate_up[l], w_down[l], b_down[l])
    y_top = jnp.take(y_rows, dest.T.reshape(-1), axis=0, mode='clip').reshape(TOP_K, T, D)
    out = _combine(h1, y_top, gates, ln2_g[l], ln2_b[l], alpha)
    return out.reshape(B, S, D)
```

```python
import functools
import math

import jax
import jax.numpy as jnp
from jax import lax
from jax.experimental import pallas as pl
from jax.experimental.pallas import tpu as pltpu

F32 = jnp.float32
BF16 = jnp.bfloat16

HEAD_DIM = 64
RWKV_HEADS = 6
RWKV_WIDTH = RWKV_HEADS * HEAD_DIM
GATE_LORA = 64
DECAY_LORA = 32
ICLR_LORA = 32
LANES = 128
RWKV_CHUNK = 64
RWKV_PAD_IN = 3 * RWKV_WIDTH + 2 * LANES
RWKV_GN_EPS = 64e-5
DECAY_SCALE = math.exp(-0.5)
VMEM_LIMIT = 56 * 1024 * 1024


def _bf(x):
    return x.astype(BF16)


def _dot(a, b):
    return jnp.dot(_bf(a), _bf(b), preferred_element_type=F32)


def _dot_nt(a, b):
    return lax.dot_general(_bf(a), _bf(b), (((1,), (1,)), ((), ())), preferred_element_type=F32)


def _dot_tn(a, b):
    return lax.dot_general(_bf(a), _bf(b), (((0,), (0,)), ((), ())), preferred_element_type=F32)


def _split3(x):
    hi = _bf(x)
    r1 = x - hi.astype(F32)
    mid = _bf(r1)
    lo = _bf(r1 - mid.astype(F32))
    return hi, mid, lo


def _dot_exact_lhs(a_bf, x, terms=3):
    parts = _split3(x)[:terms]
    acc = jnp.dot(a_bf, parts[0], preferred_element_type=F32)
    for p in parts[1:]:
        acc = acc + jnp.dot(a_bf, p, preferred_element_type=F32)
    return acc


def _dot_exact_rhs(x, b_bf, terms=2):
    parts = _split3(x)[:terms]
    acc = jnp.dot(parts[0], b_bf, preferred_element_type=F32)
    for p in parts[1:]:
        acc = acc + jnp.dot(p, b_bf, preferred_element_type=F32)
    return acc


def _sigmoid(x):
    return 1.0 / (1.0 + jnp.exp(-x))


def _rwkv_kernel(p_ref, mu_ref, wup_ref, aup_ref, gup_ref, w0_ref, a0_ref, kk_ref, ka_ref, rk_ref,
                 gng_ref, gnb_ref, hsum_ref, o_ref, state_ref, *, seq, chunk):
    C, N, H, W = chunk, HEAD_DIM, RWKV_HEADS, RWKV_WIDTH
    nc = seq // C
    c = pl.program_id(1)

    @pl.when(c == 0)
    def _():
        state_ref[...] = jnp.zeros_like(state_ref)

    hsum = hsum_ref[...]
    row = lax.broadcasted_iota(jnp.int32, (C, C), 0)
    col = lax.broadcasted_iota(jnp.int32, (C, C), 1)
    rowc = lax.broadcasted_iota(jnp.int32, (C, 1), 0)

    def shifted(ci):
        base = pl.multiple_of(ci * C, C)
        cur = p_ref[0, pl.ds(base, C), :]
        pb = p_ref[0, pl.ds(pl.multiple_of(jnp.maximum(base - 8, 0), 8), 8), :]
        nb = p_ref[0, pl.ds(pl.multiple_of(jnp.minimum(base + C, seq - 8), 8), 8), :]
        prev_row = jnp.where(ci > 0, pb[7:8, :], 0.0)
        next_row = jnp.where(ci < nc - 1, nb[0:1, :], 0.0)
        prev = jnp.where(rowc == 0, prev_row, pltpu.roll(cur, 1, 0))
        nxt = jnp.where(rowc == C - 1, next_row, pltpu.roll(cur, C - 1, 0))
        return cur + mu_ref[...] * (0.5 * (prev + nxt) - cur)

    def prepare(d, ci):
        ps = shifted(ci)
        r = ps[:, 0:W]
        k = ps[:, W:2 * W]
        v = ps[:, 2 * W:3 * W]
        x1 = ps[:, 3 * W:3 * W + LANES]
        x2 = ps[:, 3 * W + LANES:3 * W + 2 * LANES]
        a_both = [_sigmoid(a0_ref[e:e + 1, :] + _dot(x2, aup_ref[e])) for e in range(2)]
        w_logit = w0_ref[d:d + 1, :] + _dot(jnp.tanh(x1), wup_ref[d])
        logw = -DECAY_SCALE * _sigmoid(w_logit)
        kkr = k * kk_ref[...]
        ss = _dot_exact_rhs(kkr * kkr, hsum)
        kk = kkr / jnp.maximum(jnp.sqrt(ss), 1e-12)
        kdir_both = [k * (1.0 + (a - 1.0) * ka_ref[...]) for a in a_both]
        kdir = kdir_both[d]
        b = kk * a_both[d]
        incl = (row >= col) if d == 0 else (row <= col)
        tri = jnp.where(incl, 1.0, 0.0).astype(BF16)
        lin = _dot_exact_lhs(tri, logw)
        tot = jnp.sum(logw, axis=0, keepdims=True)
        e_neg = jnp.exp(-lin)
        e_tot = jnp.exp(tot)
        kh = kdir * e_neg
        bh = b * e_neg
        return dict(r=r, v=v, x1=x1, ksum=kdir_both[0] + kdir_both[1], rt=r * jnp.exp(lin),
                    at=kk * jnp.exp(lin - logw), kh=kh, bh=bh, kc=kh * e_tot, bc=bh * e_tot, e_tot=e_tot)

    chunk_of = (c, nc - 1 - c)
    pre = [prepare(d, chunk_of[d]) for d in range(2)]
    chains = [(d, h) for d in range(2) for h in range(H)]
    assert 2 * N == LANES and C == N
    r2 = lax.broadcasted_iota(jnp.int32, (2 * C, LANES), 0)
    l2 = lax.broadcasted_iota(jnp.int32, (2 * C, LANES), 1)
    t_row = jnp.where(r2 >= C, r2 - C, r2)
    t_col = jnp.where(l2 >= N, l2 - N, l2)
    before = (t_row > t_col, t_row < t_col)
    diag_r = jnp.logical_and(r2 >= C, t_row == t_col)
    tile_mask = [jnp.logical_or(before[d], diag_r) for d in range(2)]
    left = lax.broadcasted_iota(jnp.int32, (C, LANES), 1) < N
    zeros_cn = jnp.zeros((C, N), F32)
    zeros_cl = jnp.zeros((C, LANES), F32)

    s0, tile0, ws = {}, {}, {}
    for d, h in chains:
        sl = slice(h * N, (h + 1) * N)
        q = pre[d]
        lhs2 = jnp.concatenate([q['at'][:, sl], q['rt'][:, sl]], axis=0)
        s0[d, h] = state_ref[d, h]
        rhs3 = jnp.concatenate([q['kh'][:, sl], q['bh'][:, sl], s0[d, h]], axis=0)
        g = _dot_nt(lhs2, rhs3)
        tile0[d, h] = jnp.where(tile_mask[d], g[:, 0:LANES], 0.0)
        ws[d, h] = g[:, LANES:LANES + N]
    av = {}
    for d, h in chains:
        sl = slice(h * N, (h + 1) * N)
        lhs = jnp.where(l2 < N, tile0[d, h], 0.0)
        av[d, h] = _dot(lhs, jnp.concatenate([pre[d]['v'][:, sl], zeros_cn], axis=0))
    z = {}
    for d, h in chains:
        u0 = jnp.concatenate([ws[d, h][0:C] + av[d, h][0:C], zeros_cn], axis=1)
        z[d, h] = jnp.where(left, u0, -tile0[d, h][0:C])
    for _ in range(int(math.log2(C))):
        for d, h in chains:
            prod = _dot(jnp.where(left, 0.0, z[d, h]), jnp.concatenate([zeros_cl, z[d, h]], axis=0))
            z[d, h] = jnp.where(left, z[d, h] + prod, prod)
    ys = {}
    for d, h in chains:
        sl = slice(h * N, (h + 1) * N)
        q = pre[d]
        u = z[d, h][:, 0:N]
        rb = jnp.where(left, 0.0, tile0[d, h][C:2 * C])
        ys[d, h] = ws[d, h][C:2 * C] + av[d, h][C:2 * C] - _dot(rb, jnp.concatenate([zeros_cn, u], axis=0))
        upd = _dot_tn(jnp.concatenate([q['v'][:, sl], u], axis=0),
                      jnp.concatenate([q['kc'][:, sl], -q['bc'][:, sl]], axis=0))
        state_ref[d, h] = s0[d, h] * q['e_tot'][:, sl] + upd

    y = [jnp.concatenate([ys[d, h] for h in range(H)], axis=1) for d in range(2)]
    base = [pl.multiple_of(chunk_of[d] * C, C) for d in range(2)]

    @pl.when(c < nc // 2)
    def _():
        for d in range(2):
            o_ref[0, pl.ds(base[d], C), :] = y[d]

    @pl.when(c >= nc // 2)
    def _():
        inv_n = 1.0 / N
        yt = [o_ref[0, pl.ds(base[d], C), :] + y[d] for d in range(2)]
        mean = [_dot_exact_rhs(t, hsum) * inv_n for t in yt]
        yc = [t - m for t, m in zip(yt, mean)]
        var = [_dot_exact_rhs(t * t, hsum) * inv_n for t in yc]
        for d in range(2):
            q = pre[d]
            yn = yc[d] * lax.rsqrt(var[d] + RWKV_GN_EPS) * gng_ref[...] + gnb_ref[...]
            coef = _dot_exact_rhs(q['r'] * rk_ref[...] * q['ksum'], hsum)
            gate = _dot(_sigmoid(q['x1']), gup_ref[...])
            o_ref[0, pl.ds(base[d], C), :] = (yn + coef * q['v']) * gate


def _rwkv_mixer(p_pad, mu_pad, w0, w_up, a0, a_up, g_up, k_k, k_a, r_k, gn_g, gn_b):
    B, S, PW = p_pad.shape
    W = RWKV_WIDTH
    C = RWKV_CHUNK
    wup = jnp.zeros((2, LANES, W), F32)
    for d in range(2):
        lo = GATE_LORA + d * DECAY_LORA
        wup = wup.at[d, lo:lo + DECAY_LORA].set(w_up[d])
    aup = jnp.zeros((2, LANES, W), F32)
    for d in range(2):
        aup = aup.at[d, d * ICLR_LORA:(d + 1) * ICLR_LORA].set(a_up[d])
    gup = jnp.zeros((LANES, W), F32).at[:GATE_LORA].set(g_up)
    head = jnp.arange(W) // HEAD_DIM
    hsum = (head[:, None] == head[None, :]).astype(BF16)
    row = lambda t: t.reshape(1, W)
    const2 = lambda shape: pl.BlockSpec(shape, lambda b, c: (0,) * len(shape))
    return pl.pallas_call(
        functools.partial(_rwkv_kernel, seq=S, chunk=C),
        grid=(B, S // C // 1),
        in_specs=[pl.BlockSpec((1, S, PW), lambda b, c: (b, 0, 0)),
                  const2((1, PW)), const2((2, LANES, W)), const2((2, LANES, W)), const2((LANES, W)),
                  const2((2, W)), const2((2, W)), const2((1, W)), const2((1, W)), const2((1, W)),
                  const2((1, W)), const2((1, W)), const2((W, W))],
        out_specs=pl.BlockSpec((1, S, W), lambda b, c: (b, 0, 0)),
        out_shape=jax.ShapeDtypeStruct((B, S, W), F32),
        scratch_shapes=[pltpu.VMEM((2, RWKV_HEADS, HEAD_DIM, HEAD_DIM), F32)],
        compiler_params=pltpu.CompilerParams(dimension_semantics=("arbitrary", "arbitrary"),
                                             vmem_limit_bytes=VMEM_LIMIT),
        name="rwkv7_scan",
    )(p_pad, mu_pad.reshape(1, PW), _bf(wup), _bf(aup), _bf(gup), w0, a0, row(k_k), row(k_a),
      r_k.reshape(1, W), row(gn_g), row(gn_b), hsum)


ATT_HEADS = 6
ATT_WIDTH = ATT_HEADS * HEAD_DIM
DILATED_BRANCHES = ((128, 1), (512, 4), (2048, 16))
ATT_BLOCK = 64
NEG_INF = -1e30
ATT_UNROLL = 16


def _rows(start, size, stride):
    return pl.ds(start, size) if stride == 1 else pl.ds(start, size, stride=stride)


def _att_kernel(slope_ref, q_ref, k_ref, v_ref, out_ref, o_s, m_s, l_s, *, seq):
    S, Q = seq, ATT_BLOCK
    left = lax.broadcasted_iota(jnp.int32, (1, LANES), 1) < HEAD_DIM
    qi = lax.broadcasted_iota(jnp.int32, (Q, 3 * Q), 0)
    kj = lax.broadcasted_iota(jnp.int32, (Q, 3 * Q), 1)
    dist = jnp.abs(kj - Q - qi).astype(F32)
    band = dist <= Q
    for bi, (window, dil) in enumerate(DILATED_BRANCHES):
        assert window // (2 * dil) == Q and S % (Q * dil) == 0
        nb = S // (Q * dil)
        bias = [jnp.where(band, (-dil * slope_ref[0, hh:hh + 1, 0:1]) * dist, NEG_INF) for hh in range(2)]

        def body(it, carry, dil=dil, nb=nb, bias=bias, bi=bi):
            blocks = []
            for uu in range(ATT_UNROLL):
                i = it * ATT_UNROLL + uu
                n = i % nb
                start = pl.multiple_of(i * Q, Q) if dil == 1 else i // nb + dil * Q * n
                s_prev = start - jnp.where(n > 0, dil * Q, 0)
                s_next = start + jnp.where(n < nb - 1, dil * Q, 0)
                if dil == 1:
                    s_prev, s_next = pl.multiple_of(s_prev, Q), pl.multiple_of(s_next, Q)
                q = q_ref[0, _rows(start, Q, dil), :] * (HEAD_DIM ** -0.5)
                kw = _bf(jnp.concatenate([k_ref[0, _rows(s, Q, dil), :] for s in (s_prev, start, s_next)], axis=0))
                vw = _bf(jnp.concatenate([v_ref[0, _rows(s, Q, dil), :] for s in (s_prev, start, s_next)], axis=0))
                pen_prev = jnp.where(n == 0, NEG_INF, 0.0)
                pen_next = jnp.where(n == nb - 1, NEG_INF, 0.0)
                edge = jnp.where(kj < Q, pen_prev, jnp.where(kj >= 2 * Q, pen_next, 0.0))
                blocks.append((start, q, kw, vw, edge))
            scores = [[_dot_nt(jnp.where(left if hh == 0 else jnp.logical_not(left), q, 0.0), kw) + bias[hh] + edge
                       for hh in range(2)] for (_, q, kw, _, edge) in blocks]
            mx = [[jnp.max(s, axis=-1, keepdims=True) for s in ss] for ss in scores]
            pr = [[jnp.exp(s - m) for s, m in zip(ss, ms)] for ss, ms in zip(scores, mx)]
            den = [[jnp.sum(p, axis=-1, keepdims=True) for p in ps] for ps in pr]
            pv = [[jnp.dot(_bf(p), blk[3], preferred_element_type=F32) for p in ps] for ps, blk in zip(pr, blocks)]
            for uu, blk in enumerate(blocks):
                rows = _rows(blk[0], Q, dil)
                o_s[bi, rows, :] = jnp.where(left, pv[uu][0], pv[uu][1])
                m_s[bi, rows, :] = jnp.where(left, mx[uu][0], mx[uu][1])
                l_s[bi, rows, :] = jnp.where(left, den[uu][0], den[uu][1])
            return carry

        lax.fori_loop(0, S // Q // ATT_UNROLL, body, 0)

    def merge(j, carry):
        rows = pl.ds(pl.multiple_of(j * Q, Q), Q)
        ms = [m_s[b, rows, :] for b in range(3)]
        top = jnp.maximum(jnp.maximum(ms[0], ms[1]), ms[2])
        ws = [jnp.exp(m - top) for m in ms]
        num = ws[0] * o_s[0, rows, :] + ws[1] * o_s[1, rows, :] + ws[2] * o_s[2, rows, :]
        den = ws[0] * l_s[0, rows, :] + ws[1] * l_s[1, rows, :] + ws[2] * l_s[2, rows, :]
        out_ref[0, rows, :] = num / den
        return carry

    lax.fori_loop(0, S // Q, merge, 0)


def _dilated_attention(q, k, v):
    B, S, W = q.shape
    pairs = W // LANES
    slopes = jnp.exp2(-8.0 * jnp.arange(1, ATT_HEADS + 1, dtype=F32) / ATT_HEADS).reshape(pairs, 2, 1)
    slopes = jnp.broadcast_to(jnp.pad(slopes, ((0, 0), (0, 6), (0, 0))), (pairs, 8, LANES))
    spec = pl.BlockSpec((1, S, LANES), lambda b, p: (b, 0, p))
    return pl.pallas_call(
        functools.partial(_att_kernel, seq=S),
        grid=(B, pairs),
        in_specs=[pl.BlockSpec((1, 8, LANES), lambda b, p: (p, 0, 0)), spec, spec, spec],
        out_specs=spec,
        out_shape=jax.ShapeDtypeStruct((B, S, W), F32),
        scratch_shapes=[pltpu.VMEM((3, S, LANES), F32)] * 3,
        compiler_params=pltpu.CompilerParams(dimension_semantics=("arbitrary", "arbitrary"),
                                             vmem_limit_bytes=VMEM_LIMIT),
        name="dilated_attention",
    )(slopes, q, k, v)


N_EXPERTS = 32
TOP_K = 4
SWIGLU_LIMIT = 7.0
SWIGLU_ALPHA = 1.702
MOE_ROWS = 512


CAST_ROWS = 128


def _expert_kernel(be_ref, nb_ref, x_ref, wgu_ref, bgu_ref, wd_ref, bd_ref, o_ref, wgu_s, wd_s):
    i = pl.program_id(0)
    F = wd_ref.shape[1]
    used = i < nb_ref[0]

    @pl.when(used & ((i == 0) | (be_ref[i] != be_ref[jnp.maximum(i - 1, 0)])))
    def _():
        def cast(src, dst):
            def step(j, carry):
                rows = pl.ds(pl.multiple_of(j * CAST_ROWS, CAST_ROWS), CAST_ROWS)
                dst[rows, :] = _bf(src[0, rows, :])
                return carry
            lax.fori_loop(0, src.shape[1] // CAST_ROWS, step, 0)
        cast(wgu_ref, wgu_s)
        cast(wd_ref, wd_s)

    @pl.when(used)
    def _():
        gu = jnp.dot(x_ref[...], wgu_s[...], preferred_element_type=F32) + bgu_ref[0]
        gate = jnp.minimum(gu[:, :F], SWIGLU_LIMIT)
        up = jnp.clip(gu[:, F:], -SWIGLU_LIMIT, SWIGLU_LIMIT)
        act = (up + 1.0) * gate * _sigmoid(SWIGLU_ALPHA * gate)
        o_ref[...] = (jnp.dot(_bf(act), wd_s[...], preferred_element_type=F32) + bd_ref[0]).astype(o_ref.dtype)

    @pl.when(jnp.logical_not(used))
    def _():
        o_ref[...] = jnp.zeros_like(o_ref)


def _expert_blocks(block_e, n_used, x_rows, w_gate_up, b_gate_up, w_down, b_down):
    n_rows, D = x_rows.shape
    E, _, F2 = w_gate_up.shape
    F = F2 // 2
    n_blocks = n_rows // MOE_ROWS
    return pl.pallas_call(
        _expert_kernel,
        grid_spec=pltpu.PrefetchScalarGridSpec(
            num_scalar_prefetch=2,
            grid=(n_blocks,),
            in_specs=[pl.BlockSpec((MOE_ROWS, D), lambda i, be, nb: (i, 0)),
                      pl.BlockSpec((1, D, F2), lambda i, be, nb: (be[i], 0, 0)),
                      pl.BlockSpec((1, 1, F2), lambda i, be, nb: (be[i], 0, 0)),
                      pl.BlockSpec((1, F, D), lambda i, be, nb: (be[i], 0, 0)),
                      pl.BlockSpec((1, 1, D), lambda i, be, nb: (be[i], 0, 0))],
            out_specs=pl.BlockSpec((MOE_ROWS, D), lambda i, be, nb: (i, 0)),
            scratch_shapes=[pltpu.VMEM((D, F2), BF16), pltpu.VMEM((F, D), BF16)]),
        out_shape=jax.ShapeDtypeStruct((n_rows, D), BF16),
        compiler_params=pltpu.CompilerParams(dimension_semantics=("arbitrary",),
                                             vmem_limit_bytes=VMEM_LIMIT),
        name="moe_experts",
    )(block_e, n_used, x_rows, w_gate_up, b_gate_up.reshape(E, 1, F2), w_down, b_down.reshape(E, 1, D))


def _route(top_e, n_tokens):
    T = n_tokens
    n_rows = T * TOP_K + N_EXPERTS * MOE_ROWS
    n_blocks = n_rows // MOE_ROWS
    experts = jnp.arange(N_EXPERTS, dtype=jnp.int32)
    onehot = jnp.sum((top_e[:, :, None] == experts).astype(jnp.int32), axis=1)
    incl = jnp.cumsum(onehot, axis=0)
    counts = incl[-1]
    padded = (counts + MOE_ROWS - 1) // MOE_ROWS * MOE_ROWS
    starts = jnp.cumsum(counts) - counts
    pends = jnp.cumsum(padded)
    pstarts = pends - padded
    rank = jnp.take_along_axis(incl - onehot, top_e, axis=1)
    dest = pstarts[top_e] + rank
    order = jnp.argsort(top_e.reshape(-1), stable=True).astype(jnp.int32)
    block_start = jnp.arange(n_blocks, dtype=jnp.int32) * MOE_ROWS
    block_e = jnp.minimum(jnp.sum((pends[None, :] <= block_start[:, None]).astype(jnp.int32), axis=1),
                          N_EXPERTS - 1)
    row = jnp.arange(n_rows, dtype=jnp.int32)
    within = row - jnp.repeat(pstarts[block_e], MOE_ROWS)
    src = jnp.clip(within + jnp.repeat(starts[block_e], MOE_ROWS), 0, T * TOP_K - 1)
    row_token = jnp.where(within < jnp.repeat(counts[block_e], MOE_ROWS),
                          jnp.take(order, src, mode='clip') // TOP_K, row % T)
    n_used = (pends[-1:] // MOE_ROWS).astype(jnp.int32)
    return dest.astype(jnp.int32), row_token.astype(jnp.int32), block_e.astype(jnp.int32), n_used


LN_EPS = 1e-5
MEM_HEADS = 4
MEM_WIDTH = MEM_HEADS * HEAD_DIM
RWKV_IN = 3 * RWKV_WIDTH + GATE_LORA + 2 * DECAY_LORA + 2 * ICLR_LORA
PROJ_ROWS = 512


def _ln(x, g, b):
    mu = jnp.mean(x, -1, keepdims=True)
    xc = x - mu
    var = jnp.mean(xc * xc, -1, keepdims=True)
    return xc * lax.rsqrt(var + LN_EPS) * g + b


def _in_proj_kernel(x_ref, g_ref, b_ref, w_ref, rw_ref, q_ref, k_ref, v_ref, qm_ref):
    hb = _bf(_ln(x_ref[...], g_ref[...], b_ref[...]))
    o = 0
    for ref in (rw_ref, q_ref, k_ref, v_ref, qm_ref):
        n = ref.shape[-1]
        ref[...] = jnp.dot(hb, w_ref[:, o:o + n], preferred_element_type=F32)
        o += n


def _in_proj(xt, ln_g, ln_b, w_pad):
    T, D = xt.shape
    widths = (RWKV_PAD_IN, ATT_WIDTH, ATT_WIDTH, ATT_WIDTH, MEM_WIDTH)
    assert sum(widths) == w_pad.shape[1]
    rows = lambda n: pl.BlockSpec((PROJ_ROWS, n), lambda i: (i, 0))
    full = lambda a: pl.BlockSpec(a.shape, lambda i: (0,) * a.ndim)
    args = (xt, ln_g.reshape(1, D), ln_b.reshape(1, D), w_pad)
    return pl.pallas_call(
        _in_proj_kernel,
        grid=(T // PROJ_ROWS,),
        in_specs=[rows(D)] + [full(a) for a in args[1:]],
        out_specs=[rows(n) for n in widths],
        out_shape=[jax.ShapeDtypeStruct((T, n), F32) for n in widths],
        compiler_params=pltpu.CompilerParams(dimension_semantics=("arbitrary",), vmem_limit_bytes=VMEM_LIMIT),
        name="in_proj",
    )(*args)


def _mem_att_kernel(q_ref, mem_ref, wkv_ref, o_ref, kv_s):
    W = MEM_WIDTH

    @pl.when(pl.program_id(1) == 0)
    def _():
        kv_s[...] = _bf(jnp.dot(_bf(mem_ref[0]), wkv_ref[...], preferred_element_type=F32))

    q = q_ref[0] * (HEAD_DIM ** -0.5)
    lane = lax.broadcasted_iota(jnp.int32, (1, W), 1)
    kmat = kv_s[:, 0:W]
    vmat = kv_s[:, W:2 * W]
    out = jnp.zeros(q.shape, F32)
    for h in range(MEM_HEADS):
        sel = (lane >= h * HEAD_DIM) & (lane < (h + 1) * HEAD_DIM)
        s = lax.dot_general(_bf(jnp.where(sel, q, 0.0)), kmat, (((1,), (1,)), ((), ())),
                            preferred_element_type=F32)
        p = jnp.exp(s - jnp.max(s, axis=-1, keepdims=True))
        pv = jnp.dot(_bf(p), vmat, preferred_element_type=F32)
        out = jnp.where(sel, pv / jnp.sum(p, axis=-1, keepdims=True), out)
    o_ref[0] = out


def _mem_attention(q_m, mem, w_kv):
    B, S, W = q_m.shape
    _, M, D = mem.shape
    return pl.pallas_call(
        _mem_att_kernel,
        grid=(B, S // PROJ_ROWS),
        in_specs=[pl.BlockSpec((1, PROJ_ROWS, W), lambda b, i: (b, i, 0)),
                  pl.BlockSpec((1, M, D), lambda b, i: (b, 0, 0)),
                  pl.BlockSpec((D, 2 * W), lambda b, i: (0, 0))],
        out_specs=pl.BlockSpec((1, PROJ_ROWS, W), lambda b, i: (b, i, 0)),
        out_shape=jax.ShapeDtypeStruct((B, S, W), F32),
        scratch_shapes=[pltpu.VMEM((M, 2 * W), BF16)],
        compiler_params=pltpu.CompilerParams(dimension_semantics=("arbitrary", "arbitrary"),
                                             vmem_limit_bytes=VMEM_LIMIT),
        name="memory_attention",
    )(q_m, mem, _bf(w_kv))


def _out_proj_kernel(x_ref, yr_ref, ya_ref, ym_ref, g0_ref, b0_ref, w_ref, g1_ref, b1_ref, wrh_ref, wrl_ref,
                     br_ref, h1_ref, h1b_ref, gate_ref, expert_ref, *, alpha):
    h0 = _ln(x_ref[...], g0_ref[...], b0_ref[...])
    o = 0
    mix = None
    for ref in (yr_ref, ya_ref, ym_ref):
        n = ref.shape[-1]
        part = jnp.dot(_bf(ref[...]), w_ref[o:o + n, :], preferred_element_type=F32)
        mix = part if mix is None else mix + part
        o += n
    h1 = _ln(alpha * h0 + mix, g1_ref[...], b1_ref[...])
    h1_ref[...] = h1
    hi = _bf(h1)
    h1b_ref[...] = hi
    lo = _bf(h1 - hi.astype(F32))
    logits = (jnp.dot(hi, wrh_ref[...], preferred_element_type=F32)
              + jnp.dot(lo, wrh_ref[...], preferred_element_type=F32)
              + jnp.dot(hi, wrl_ref[...], preferred_element_type=F32)) + br_ref[...]
    lane = lax.broadcasted_iota(jnp.int32, logits.shape, 1).astype(F32)
    vals = logits
    top_v, top_i = [], []
    for _ in range(TOP_K):
        m = jnp.max(vals, axis=-1, keepdims=True)
        idx = jnp.min(jnp.where(vals == m, lane, float(LANES)), axis=-1, keepdims=True)
        top_v.append(m)
        top_i.append(idx)
        vals = jnp.where(lane == idx, -jnp.inf, vals)
    ex = [jnp.exp(v - top_v[0]) for v in top_v]
    den = ex[0] + ex[1] + ex[2] + ex[3]
    gates = jnp.zeros(logits.shape, F32)
    experts = jnp.zeros(logits.shape, F32)
    for k in range(TOP_K):
        gates = jnp.where(lane == k, ex[k] / den, gates)
        experts = jnp.where(lane == k, top_i[k], experts)
    gate_ref[...] = gates
    expert_ref[...] = experts.astype(jnp.int32)


def _out_proj(xt, y_rwkv, y_att, y_mem, ln0_g, ln0_b, w_out, ln1_g, ln1_b, w_router, b_router, alpha):
    T, D = xt.shape
    E = w_router.shape[1]
    wr = jnp.pad(w_router, ((0, 0), (0, LANES - E)))
    wr_hi = _bf(wr)
    wr_lo = _bf(wr - wr_hi.astype(F32))
    br = jnp.pad(b_router, (0, LANES - E), constant_values=NEG_INF).reshape(1, LANES)
    rows = lambda n: pl.BlockSpec((PROJ_ROWS, n), lambda i: (i, 0))
    full = lambda a: pl.BlockSpec(a.shape, lambda i: (0,) * a.ndim)
    vec = lambda t: t.reshape(1, D)
    consts = (vec(ln0_g), vec(ln0_b), _bf(w_out), vec(ln1_g), vec(ln1_b), wr_hi, wr_lo, br)
    return pl.pallas_call(
        functools.partial(_out_proj_kernel, alpha=alpha),
        grid=(T // PROJ_ROWS,),
        in_specs=[rows(D), rows(y_rwkv.shape[1]), rows(y_att.shape[1]), rows(y_mem.shape[1])]
                 + [full(a) for a in consts],
        out_specs=[rows(D), rows(D), rows(LANES), rows(LANES)],
        out_shape=[jax.ShapeDtypeStruct((T, D), F32), jax.ShapeDtypeStruct((T, D), BF16),
                   jax.ShapeDtypeStruct((T, LANES), F32), jax.ShapeDtypeStruct((T, LANES), jnp.int32)],
        compiler_params=pltpu.CompilerParams(dimension_semantics=("arbitrary",), vmem_limit_bytes=VMEM_LIMIT),
        name="out_proj_router",
    )(xt, y_rwkv, y_att, y_mem, *consts)


def _combine_kernel(h1_ref, y_ref, gate_ref, g_ref, b_ref, o_ref, *, alpha):
    acc = alpha * h1_ref[...]
    gates = gate_ref[...]
    for k in range(TOP_K):
        acc = acc + gates[:, k:k + 1] * y_ref[k].astype(F32)
    o_ref[...] = _ln(acc, g_ref[...], b_ref[...])


def _combine(h1, y_top, gates, ln_g, ln_b, alpha):
    T, D = h1.shape
    rows = lambda n: pl.BlockSpec((PROJ_ROWS, n), lambda i: (i, 0))
    full = lambda a: pl.BlockSpec(a.shape, lambda i: (0,) * a.ndim)
    consts = (ln_g.reshape(1, D), ln_b.reshape(1, D))
    return pl.pallas_call(
        functools.partial(_combine_kernel, alpha=alpha),
        grid=(T // PROJ_ROWS,),
        in_specs=[rows(D), pl.BlockSpec((TOP_K, PROJ_ROWS, D), lambda i: (0, i, 0)), rows(LANES)]
                 + [full(a) for a in consts],
        out_specs=rows(D),
        out_shape=jax.ShapeDtypeStruct((T, D), F32),
        compiler_params=pltpu.CompilerParams(dimension_semantics=("arbitrary",), vmem_limit_bytes=VMEM_LIMIT),
        name="combine_ln",
    )(h1, y_top, gates, *consts)


def kernel(x, mem, ln_in_g, ln_in_b, w_in, mu_shift, w0, w_up, a0, a_up, g_up, k_k, k_a, r_k, gn_g, gn_b, w_mem_kv, w_out, ln1_g, ln1_b, w_router, b_router, w_gate_up, b_gate_up, w_down, b_down, ln2_g, ln2_b):
    B, S, D = x.shape
    T = B * S
    depth = w_in.shape[0]
    assert depth == 1, "the layer norm feeding a layer is fused into its projection kernels"
    alpha = (2 * depth) ** 0.25
    l = 0
    xt = x.reshape(T, D)
    pad = RWKV_PAD_IN - RWKV_IN
    w_pad = _bf(jnp.concatenate([w_in[l][:, :RWKV_IN], jnp.zeros((D, pad), F32), w_in[l][:, RWKV_IN:]], axis=1))
    mu_pad = jnp.pad(mu_shift[l], (0, pad))
    p_rwkv, q_a, k_a_, v_a, q_m = _in_proj(xt, ln_in_g, ln_in_b, w_pad)
    seq = lambda t: t.reshape(B, S, t.shape[-1])
    y_rwkv = _rwkv_mixer(seq(p_rwkv), mu_pad, w0[l], w_up[l], a0[l], a_up[l], g_up[l], k_k[l], k_a[l], r_k[l],
                         gn_g[l], gn_b[l])
    y_att = _dilated_attention(seq(q_a), seq(k_a_), seq(v_a))
    y_mem = _mem_attention(seq(q_m), mem, w_mem_kv[l])
    flat = lambda t: t.reshape(T, t.shape[-1])
    h1, h1b, gates, experts = _out_proj(xt, flat(y_rwkv), flat(y_att), flat(y_mem), ln_in_g, ln_in_b, w_out[l],
                                        ln1_g[l], ln1_b[l], w_router[l], b_router[l], alpha)
    dest, row_token, block_e, n_used = _route(experts[:, :TOP_K], T)
    x_rows = jnp.take(h1b, row_token, axis=0, mode='clip')
    y_rows = _expert_blocks(block_e, n_used, x_rows, w_gate_up[l], b_gate_up[l], w_down[l], b_down[l])
    y_top = jnp.take(y_rows, dest.T.reshape(-1), axis=0, mode='clip').reshape(TOP_K, T, D)
    out = _combine(h1, y_top, gates, ln2_g[l], ln2_b[l], alpha)
    return out.reshape(B, S, D)
```

```python
import functools
import math

import jax
import jax.numpy as jnp
from jax import lax
from jax.experimental import pallas as pl
from jax.experimental.pallas import tpu as pltpu

F32 = jnp.float32
BF16 = jnp.bfloat16

HEAD_DIM = 64
RWKV_HEADS = 6
RWKV_WIDTH = RWKV_HEADS * HEAD_DIM
GATE_LORA = 64
DECAY_LORA = 32
ICLR_LORA = 32
LANES = 128
RWKV_CHUNK = 64
RWKV_BATCH = 2
RWKV_PAD_IN = 3 * RWKV_WIDTH + 2 * LANES
RWKV_GN_EPS = 64e-5
DECAY_SCALE = math.exp(-0.5)
VMEM_LIMIT = 56 * 1024 * 1024


def _bf(x):
    return x.astype(BF16)


def _dot(a, b):
    return jnp.dot(_bf(a), _bf(b), preferred_element_type=F32)


def _dot_nt(a, b):
    return lax.dot_general(_bf(a), _bf(b), (((1,), (1,)), ((), ())), preferred_element_type=F32)


def _dot_tn(a, b):
    return lax.dot_general(_bf(a), _bf(b), (((0,), (0,)), ((), ())), preferred_element_type=F32)


def _split3(x):
    hi = _bf(x)
    r1 = x - hi.astype(F32)
    mid = _bf(r1)
    lo = _bf(r1 - mid.astype(F32))
    return hi, mid, lo


def _dot_exact_lhs(a_bf, x, terms=3):
    parts = _split3(x)[:terms]
    acc = jnp.dot(a_bf, parts[0], preferred_element_type=F32)
    for p in parts[1:]:
        acc = acc + jnp.dot(a_bf, p, preferred_element_type=F32)
    return acc


def _dot_exact_rhs(x, b_bf, terms=2):
    parts = _split3(x)[:terms]
    acc = jnp.dot(parts[0], b_bf, preferred_element_type=F32)
    for p in parts[1:]:
        acc = acc + jnp.dot(p, b_bf, preferred_element_type=F32)
    return acc


def _sigmoid(x):
    return 1.0 / (1.0 + jnp.exp(-x))


def _rwkv_kernel(pf_ref, pb_ref, wup_ref, aup_ref, gup_ref, w0_ref, a0_ref, kk_ref, ka_ref, rk_ref,
                 gng_ref, gnb_ref, hsum_ref, o_ref, state_ref, *, seq, chunk):
    C, N, H, W = chunk, HEAD_DIM, RWKV_HEADS, RWKV_WIDTH
    nc = seq // C
    c = pl.program_id(1)

    @pl.when(c == 0)
    def _():
        state_ref[...] = jnp.zeros_like(state_ref)

    hsum = hsum_ref[...]
    row = lax.broadcasted_iota(jnp.int32, (C, C), 0)
    col = lax.broadcasted_iota(jnp.int32, (C, C), 1)

    chunk_of = (c, nc - 1 - c)
    groups = [(bb, d) for d in range(2) for bb in range(RWKV_BATCH)]
    G = len(groups)
    rows_of = {g: slice(i * C, (i + 1) * C) for i, g in enumerate(groups)}
    stack = lambda parts: jnp.concatenate(parts, axis=0)

    def prepare(bb, d):
        ps = (pf_ref if d == 0 else pb_ref)[bb]
        r = ps[:, 0:W]
        k = ps[:, W:2 * W]
        v = ps[:, 2 * W:3 * W]
        x1 = ps[:, 3 * W:3 * W + LANES]
        x2 = ps[:, 3 * W + LANES:3 * W + 2 * LANES]
        a_both = [_sigmoid(a0_ref[e:e + 1, :] + _dot(x2, aup_ref[e])) for e in range(2)]
        w_logit = w0_ref[d:d + 1, :] + _dot(jnp.tanh(x1), wup_ref[d])
        logw = -DECAY_SCALE * _sigmoid(w_logit)
        kkr = k * kk_ref[...]
        ss = _dot_exact_rhs(kkr * kkr, hsum)
        kk = kkr / jnp.maximum(jnp.sqrt(ss), 1e-12)
        kdir_both = [k * (1.0 + (a - 1.0) * ka_ref[...]) for a in a_both]
        kdir = kdir_both[d]
        b = kk * a_both[d]
        incl = (row >= col) if d == 0 else (row <= col)
        tri = jnp.where(incl, 1.0, 0.0).astype(BF16)
        lin = _dot_exact_lhs(tri, logw)
        tot = jnp.sum(logw, axis=0, keepdims=True)
        e_neg = jnp.exp(-lin)
        e_tot = jnp.exp(tot)
        kh = kdir * e_neg
        bh = b * e_neg
        return dict(r=r, v=v, x1=x1, ksum=kdir_both[0] + kdir_both[1], rt=r * jnp.exp(lin),
                    at=kk * jnp.exp(lin - logw), kh=kh, bh=bh, kc=kh * e_tot, bc=bh * e_tot, e_tot=e_tot)

    pre = {g: prepare(*g) for g in groups}
    chains = [(bb, d, h) for bb, d in groups for h in range(H)]
    assert 2 * N == LANES and C == N
    r2 = lax.broadcasted_iota(jnp.int32, (2 * C, LANES), 0)
    l2 = lax.broadcasted_iota(jnp.int32, (2 * C, LANES), 1)
    t_row = jnp.where(r2 >= C, r2 - C, r2)
    t_col = jnp.where(l2 >= N, l2 - N, l2)
    before = (t_row > t_col, t_row < t_col)
    diag_r = jnp.logical_and(r2 >= C, t_row == t_col)
    tile_mask = [jnp.logical_or(before[d], diag_r) for d in range(2)]
    left = lax.broadcasted_iota(jnp.int32, (C, LANES), 1) < N
    zeros_cn = jnp.zeros((C, N), F32)
    zeros_cl = jnp.zeros((C, LANES), F32)

    s0, tile0, ws = {}, {}, {}
    for bb, d, h in chains:
        sl = slice(h * N, (h + 1) * N)
        q = pre[bb, d]
        lhs2 = jnp.concatenate([q['at'][:, sl], q['rt'][:, sl]], axis=0)
        s0[bb, d, h] = state_ref[bb, d, h]
        rhs3 = jnp.concatenate([q['kh'][:, sl], q['bh'][:, sl], s0[bb, d, h]], axis=0)
        g = _dot_nt(lhs2, rhs3)
        tile0[bb, d, h] = jnp.where(tile_mask[d], g[:, 0:LANES], 0.0)
        ws[bb, d, h] = g[:, LANES:LANES + N]
    av = {}
    for bb, d, h in chains:
        sl = slice(h * N, (h + 1) * N)
        lhs = jnp.where(l2 < N, tile0[bb, d, h], 0.0)
        av[bb, d, h] = _dot(lhs, jnp.concatenate([pre[bb, d]['v'][:, sl], zeros_cn], axis=0))
    z = {}
    for ch in chains:
        u0 = jnp.concatenate([ws[ch][0:C] + av[ch][0:C], zeros_cn], axis=1)
        z[ch] = jnp.where(left, u0, -tile0[ch][0:C])
    for _ in range(int(math.log2(C))):
        for ch in chains:
            prod = _dot(jnp.where(left, 0.0, z[ch]), jnp.concatenate([zeros_cl, z[ch]], axis=0))
            z[ch] = jnp.where(left, z[ch] + prod, prod)
    ys = {}
    for bb, d, h in chains:
        ch = (bb, d, h)
        sl = slice(h * N, (h + 1) * N)
        q = pre[bb, d]
        u = z[ch][:, 0:N]
        rb = jnp.where(left, 0.0, tile0[ch][C:2 * C])
        ys[ch] = ws[ch][C:2 * C] + av[ch][C:2 * C] - _dot(rb, jnp.concatenate([zeros_cn, u], axis=0))
        upd = _dot_tn(jnp.concatenate([q['v'][:, sl], u], axis=0),
                      jnp.concatenate([q['kc'][:, sl], -q['bc'][:, sl]], axis=0))
        state_ref[bb, d, h] = s0[ch] * q['e_tot'][:, sl] + upd

    y = {g: jnp.concatenate([ys[g + (h,)] for h in range(H)], axis=1) for g in groups}
    base = [pl.multiple_of(chunk_of[d] * C, C) for d in range(2)]

    @pl.when(c < nc // 2)
    def _():
        for bb, d in groups:
            o_ref[bb, pl.ds(base[d], C), :] = y[bb, d]

    @pl.when(c >= nc // 2)
    def _():
        inv_n = 1.0 / N
        yt = stack([o_ref[bb, pl.ds(base[d], C), :] + y[bb, d] for bb, d in groups])
        bonus_in = stack([pre[g]['r'] * rk_ref[...] * pre[g]['ksum'] for g in groups])
        sums = _dot_exact_rhs(stack([yt, bonus_in]), hsum)
        yc = yt - sums[0:G * C] * inv_n
        coef = sums[G * C:2 * G * C]
        var = _dot_exact_rhs(yc * yc, hsum) * inv_n
        yn = yc * lax.rsqrt(var + RWKV_GN_EPS) * gng_ref[...] + gnb_ref[...]
        gate = _dot(_sigmoid(stack([pre[g]['x1'] for g in groups])), gup_ref[...])
        v_all = stack([pre[g]['v'] for g in groups])
        out = (yn + coef * v_all) * gate
        for bb, d in groups:
            o_ref[bb, pl.ds(base[d], C), :] = out[rows_of[bb, d]]


def _rwkv_mixer(p_shift, w0, w_up, a0, a_up, g_up, k_k, k_a, r_k, gn_g, gn_b):
    B, S, PW = p_shift.shape
    W = RWKV_WIDTH
    C = RWKV_CHUNK
    BB = RWKV_BATCH
    nc = S // C
    assert B % BB == 0 and nc % 2 == 0
    wup = jnp.zeros((2, LANES, W), F32)
    for d in range(2):
        lo = GATE_LORA + d * DECAY_LORA
        wup = wup.at[d, lo:lo + DECAY_LORA].set(w_up[d])
    aup = jnp.zeros((2, LANES, W), F32)
    for d in range(2):
        aup = aup.at[d, d * ICLR_LORA:(d + 1) * ICLR_LORA].set(a_up[d])
    gup = jnp.zeros((LANES, W), F32).at[:GATE_LORA].set(g_up)
    head = jnp.arange(W) // HEAD_DIM
    hsum = (head[:, None] == head[None, :]).astype(BF16)
    row = lambda t: t.reshape(1, W)
    const2 = lambda shape: pl.BlockSpec(shape, lambda b, c: (0,) * len(shape))
    return pl.pallas_call(
        functools.partial(_rwkv_kernel, seq=S, chunk=C),
        grid=(B // BB, nc),
        in_specs=[pl.BlockSpec((BB, C, PW), lambda b, c: (b, c, 0)),
                  pl.BlockSpec((BB, C, PW), lambda b, c: (b, nc - 1 - c, 0)),
                  const2((2, LANES, W)), const2((2, LANES, W)), const2((LANES, W)),
                  const2((2, W)), const2((2, W)), const2((1, W)), const2((1, W)), const2((1, W)),
                  const2((1, W)), const2((1, W)), const2((W, W))],
        out_specs=pl.BlockSpec((BB, S, W), lambda b, c: (b, 0, 0)),
        out_shape=jax.ShapeDtypeStruct((B, S, W), F32),
        scratch_shapes=[pltpu.VMEM((BB, 2, RWKV_HEADS, HEAD_DIM, HEAD_DIM), F32)],
        compiler_params=pltpu.CompilerParams(dimension_semantics=("arbitrary", "arbitrary"),
                                             vmem_limit_bytes=VMEM_LIMIT),
        name="rwkv7_scan",
    )(p_shift, p_shift, _bf(wup), _bf(aup), _bf(gup), w0, a0, row(k_k), row(k_a),
      r_k.reshape(1, W), row(gn_g), row(gn_b), hsum)


ATT_HEADS = 6
ATT_WIDTH = ATT_HEADS * HEAD_DIM
DILATED_BRANCHES = ((128, 1), (512, 4), (2048, 16))
ATT_BLOCK = 64
NEG_INF = -1e30
ATT_UNROLL = 16


def _rows(start, size, stride):
    return pl.ds(start, size) if stride == 1 else pl.ds(start, size, stride=stride)


def _att_kernel(slope_ref, q_ref, k_ref, v_ref, out_ref, o_s, m_s, l_s, *, seq):
    S, Q = seq, ATT_BLOCK
    left = lax.broadcasted_iota(jnp.int32, (1, LANES), 1) < HEAD_DIM
    qi = lax.broadcasted_iota(jnp.int32, (Q, 3 * Q), 0)
    kj = lax.broadcasted_iota(jnp.int32, (Q, 3 * Q), 1)
    dist = jnp.abs(kj - Q - qi).astype(F32)
    band = dist <= Q
    for bi, (window, dil) in enumerate(DILATED_BRANCHES):
        assert window // (2 * dil) == Q and S % (Q * dil) == 0
        nb = S // (Q * dil)
        bias = [jnp.where(band, (-dil * slope_ref[0, hh:hh + 1, 0:1]) * dist, NEG_INF) for hh in range(2)]

        def body(it, carry, dil=dil, nb=nb, bias=bias, bi=bi):
            blocks = []
            for uu in range(ATT_UNROLL):
                i = it * ATT_UNROLL + uu
                n = i % nb
                start = pl.multiple_of(i * Q, Q) if dil == 1 else i // nb + dil * Q * n
                s_prev = start - jnp.where(n > 0, dil * Q, 0)
                s_next = start + jnp.where(n < nb - 1, dil * Q, 0)
                if dil == 1:
                    s_prev, s_next = pl.multiple_of(s_prev, Q), pl.multiple_of(s_next, Q)
                q = q_ref[0, _rows(start, Q, dil), :] * (HEAD_DIM ** -0.5)
                kw = _bf(jnp.concatenate([k_ref[0, _rows(s, Q, dil), :] for s in (s_prev, start, s_next)], axis=0))
                vw = _bf(jnp.concatenate([v_ref[0, _rows(s, Q, dil), :] for s in (s_prev, start, s_next)], axis=0))
                pen_prev = jnp.where(n == 0, NEG_INF, 0.0)
                pen_next = jnp.where(n == nb - 1, NEG_INF, 0.0)
                edge = jnp.where(kj < Q, pen_prev, jnp.where(kj >= 2 * Q, pen_next, 0.0))
                blocks.append((start, q, kw, vw, edge))
            scores = [[_dot_nt(jnp.where(left if hh == 0 else jnp.logical_not(left), q, 0.0), kw) + bias[hh] + edge
                       for hh in range(2)] for (_, q, kw, _, edge) in blocks]
            mx = [[jnp.max(s, axis=-1, keepdims=True) for s in ss] for ss in scores]
            pr = [[jnp.exp(s - m) for s, m in zip(ss, ms)] for ss, ms in zip(scores, mx)]
            den = [[jnp.sum(p, axis=-1, keepdims=True) for p in ps] for ps in pr]
            pv = [[jnp.dot(_bf(p), blk[3], preferred_element_type=F32) for p in ps] for ps, blk in zip(pr, blocks)]
            for uu, blk in enumerate(blocks):
                rows = _rows(blk[0], Q, dil)
                o_s[bi, rows, :] = jnp.where(left, pv[uu][0], pv[uu][1])
                m_s[bi, rows, :] = jnp.where(left, mx[uu][0], mx[uu][1])
                l_s[bi, rows, :] = jnp.where(left, den[uu][0], den[uu][1])
            return carry

        lax.fori_loop(0, S // Q // ATT_UNROLL, body, 0)

    def merge(j, carry):
        rows = pl.ds(pl.multiple_of(j * Q, Q), Q)
        ms = [m_s[b, rows, :] for b in range(3)]
        top = jnp.maximum(jnp.maximum(ms[0], ms[1]), ms[2])
        ws = [jnp.exp(m - top) for m in ms]
        num = ws[0] * o_s[0, rows, :] + ws[1] * o_s[1, rows, :] + ws[2] * o_s[2, rows, :]
        den = ws[0] * l_s[0, rows, :] + ws[1] * l_s[1, rows, :] + ws[2] * l_s[2, rows, :]
        out_ref[0, rows, :] = num / den
        return carry

    lax.fori_loop(0, S // Q, merge, 0)


def _dilated_attention(q, k, v):
    B, S, W = q.shape
    pairs = W // LANES
    slopes = jnp.exp2(-8.0 * jnp.arange(1, ATT_HEADS + 1, dtype=F32) / ATT_HEADS).reshape(pairs, 2, 1)
    slopes = jnp.broadcast_to(jnp.pad(slopes, ((0, 0), (0, 6), (0, 0))), (pairs, 8, LANES))
    spec = pl.BlockSpec((1, S, LANES), lambda b, p: (b, 0, p))
    return pl.pallas_call(
        functools.partial(_att_kernel, seq=S),
        grid=(B, pairs),
        in_specs=[pl.BlockSpec((1, 8, LANES), lambda b, p: (p, 0, 0)), spec, spec, spec],
        out_specs=spec,
        out_shape=jax.ShapeDtypeStruct((B, S, W), F32),
        scratch_shapes=[pltpu.VMEM((3, S, LANES), F32)] * 3,
        compiler_params=pltpu.CompilerParams(dimension_semantics=("arbitrary", "arbitrary"),
                                             vmem_limit_bytes=VMEM_LIMIT),
        name="dilated_attention",
    )(slopes, q, k, v)


N_EXPERTS = 32
TOP_K = 4
SWIGLU_LIMIT = 7.0
SWIGLU_ALPHA = 1.702
MOE_ROWS = 512


CAST_ROWS = 128


def _expert_kernel(be_ref, nb_ref, x_ref, wgu_ref, bgu_ref, wd_ref, bd_ref, o_ref, wgu_s, wd_s):
    i = pl.program_id(0)
    F = wd_ref.shape[1]
    used = i < nb_ref[0]

    @pl.when(used & ((i == 0) | (be_ref[i] != be_ref[jnp.maximum(i - 1, 0)])))
    def _():
        def cast(src, dst):
            def step(j, carry):
                rows = pl.ds(pl.multiple_of(j * CAST_ROWS, CAST_ROWS), CAST_ROWS)
                dst[rows, :] = _bf(src[0, rows, :])
                return carry
            lax.fori_loop(0, src.shape[1] // CAST_ROWS, step, 0)
        cast(wgu_ref, wgu_s)
        cast(wd_ref, wd_s)

    @pl.when(used)
    def _():
        gu = jnp.dot(x_ref[...], wgu_s[...], preferred_element_type=F32) + bgu_ref[0]
        gate = jnp.minimum(gu[:, :F], SWIGLU_LIMIT)
        up = jnp.clip(gu[:, F:], -SWIGLU_LIMIT, SWIGLU_LIMIT)
        act = (up + 1.0) * gate * _sigmoid(SWIGLU_ALPHA * gate)
        o_ref[...] = (jnp.dot(_bf(act), wd_s[...], preferred_element_type=F32) + bd_ref[0]).astype(o_ref.dtype)

    @pl.when(jnp.logical_not(used))
    def _():
        o_ref[...] = jnp.zeros_like(o_ref)


def _expert_blocks(block_e, n_used, x_rows, w_gate_up, b_gate_up, w_down, b_down):
    n_rows, D = x_rows.shape
    E, _, F2 = w_gate_up.shape
    F = F2 // 2
    n_blocks = n_rows // MOE_ROWS
    return pl.pallas_call(
        _expert_kernel,
        grid_spec=pltpu.PrefetchScalarGridSpec(
            num_scalar_prefetch=2,
            grid=(n_blocks,),
            in_specs=[pl.BlockSpec((MOE_ROWS, D), lambda i, be, nb: (i, 0)),
                      pl.BlockSpec((1, D, F2), lambda i, be, nb: (be[i], 0, 0)),
                      pl.BlockSpec((1, 1, F2), lambda i, be, nb: (be[i], 0, 0)),
                      pl.BlockSpec((1, F, D), lambda i, be, nb: (be[i], 0, 0)),
                      pl.BlockSpec((1, 1, D), lambda i, be, nb: (be[i], 0, 0))],
            out_specs=pl.BlockSpec((MOE_ROWS, D), lambda i, be, nb: (i, 0)),
            scratch_shapes=[pltpu.VMEM((D, F2), BF16), pltpu.VMEM((F, D), BF16)]),
        out_shape=jax.ShapeDtypeStruct((n_rows, D), BF16),
        compiler_params=pltpu.CompilerParams(dimension_semantics=("arbitrary",),
                                             vmem_limit_bytes=VMEM_LIMIT),
        name="moe_experts",
    )(block_e, n_used, x_rows, w_gate_up, b_gate_up.reshape(E, 1, F2), w_down, b_down.reshape(E, 1, D))


def _route(top_e, n_tokens):
    T = n_tokens
    n_rows = T * TOP_K + N_EXPERTS * MOE_ROWS
    n_blocks = n_rows // MOE_ROWS
    experts = jnp.arange(N_EXPERTS, dtype=jnp.int32)
    onehot = jnp.sum((top_e[:, :, None] == experts).astype(jnp.int32), axis=1)
    incl = jnp.cumsum(onehot, axis=0)
    counts = incl[-1]
    padded = (counts + MOE_ROWS - 1) // MOE_ROWS * MOE_ROWS
    starts = jnp.cumsum(counts) - counts
    pends = jnp.cumsum(padded)
    pstarts = pends - padded
    rank = jnp.take_along_axis(incl - onehot, top_e, axis=1)
    dest = pstarts[top_e] + rank
    order = jnp.argsort(top_e.reshape(-1), stable=True).astype(jnp.int32)
    block_start = jnp.arange(n_blocks, dtype=jnp.int32) * MOE_ROWS
    block_e = jnp.minimum(jnp.sum((pends[None, :] <= block_start[:, None]).astype(jnp.int32), axis=1),
                          N_EXPERTS - 1)
    row = jnp.arange(n_rows, dtype=jnp.int32)
    within = row - jnp.repeat(pstarts[block_e], MOE_ROWS)
    src = jnp.clip(within + jnp.repeat(starts[block_e], MOE_ROWS), 0, T * TOP_K - 1)
    row_token = jnp.where(within < jnp.repeat(counts[block_e], MOE_ROWS),
                          jnp.take(order, src, mode='clip') // TOP_K, row % T)
    n_used = (pends[-1:] // MOE_ROWS).astype(jnp.int32)
    return dest.astype(jnp.int32), row_token.astype(jnp.int32), block_e.astype(jnp.int32), n_used


LN_EPS = 1e-5
MEM_HEADS = 4
MEM_WIDTH = MEM_HEADS * HEAD_DIM
RWKV_IN = 3 * RWKV_WIDTH + GATE_LORA + 2 * DECAY_LORA + 2 * ICLR_LORA
PROJ_ROWS = 512


def _ln(x, g, b):
    mu = jnp.mean(x, -1, keepdims=True)
    xc = x - mu
    var = jnp.mean(xc * xc, -1, keepdims=True)
    return xc * lax.rsqrt(var + LN_EPS) * g + b


HALO = 8


def _in_proj_kernel(x_ref, xp_ref, xn_ref, g_ref, b_ref, w_ref, mu_ref, rw_ref, q_ref, k_ref, v_ref, qm_ref, *, seq):
    R = PROJ_ROWS
    i = pl.program_id(0)
    xs = jnp.concatenate([x_ref[...], xp_ref[...], xn_ref[...]], axis=0)
    hb = _bf(_ln(xs, g_ref[...], b_ref[...]))
    n_rw = rw_ref.shape[-1]
    p = jnp.dot(hb, w_ref[:, 0:n_rw], preferred_element_type=F32)
    cur = p[0:R]
    tiles_per_seq = seq // R
    first = (i % tiles_per_seq) == 0
    last = (i % tiles_per_seq) == tiles_per_seq - 1
    prev_row = jnp.where(first, 0.0, p[R + HALO - 1:R + HALO])
    next_row = jnp.where(last, 0.0, p[R + HALO:R + HALO + 1])
    rowi = lax.broadcasted_iota(jnp.int32, (R, 1), 0)
    prev = jnp.where(rowi == 0, prev_row, pltpu.roll(cur, 1, 0))
    nxt = jnp.where(rowi == R - 1, next_row, pltpu.roll(cur, R - 1, 0))
    rw_ref[...] = cur + mu_ref[...] * (0.5 * (prev + nxt) - cur)
    o = n_rw
    hb_main = hb[0:R]
    for ref in (q_ref, k_ref, v_ref, qm_ref):
        n = ref.shape[-1]
        ref[...] = jnp.dot(hb_main, w_ref[:, o:o + n], preferred_element_type=F32)
        o += n


def _in_proj(xt, ln_g, ln_b, w_pad, mu_pad, seq):
    T, D = xt.shape
    widths = (RWKV_PAD_IN, ATT_WIDTH, ATT_WIDTH, ATT_WIDTH, MEM_WIDTH)
    assert sum(widths) == w_pad.shape[1] and seq % PROJ_ROWS == 0
    per_tile = PROJ_ROWS // HALO
    last_halo = T // HALO - 1
    rows = lambda n: pl.BlockSpec((PROJ_ROWS, n), lambda i: (i, 0))
    full = lambda a: pl.BlockSpec(a.shape, lambda i: (0,) * a.ndim)
    consts = (ln_g.reshape(1, D), ln_b.reshape(1, D), w_pad, mu_pad.reshape(1, RWKV_PAD_IN))
    return pl.pallas_call(
        functools.partial(_in_proj_kernel, seq=seq),
        grid=(T // PROJ_ROWS,),
        in_specs=[rows(D),
                  pl.BlockSpec((HALO, D), lambda i: (jnp.maximum(i * per_tile - 1, 0), 0)),
                  pl.BlockSpec((HALO, D), lambda i: (jnp.minimum((i + 1) * per_tile, last_halo), 0))]
                 + [full(a) for a in consts],
        out_specs=[rows(n) for n in widths],
        out_shape=[jax.ShapeDtypeStruct((T, n), F32) for n in widths],
        compiler_params=pltpu.CompilerParams(dimension_semantics=("arbitrary",), vmem_limit_bytes=VMEM_LIMIT),
        name="in_proj",
    )(xt, xt, xt, *consts)


def _mem_att_kernel(q_ref, mem_ref, wkv_ref, o_ref, kv_s):
    W = MEM_WIDTH

    @pl.when(pl.program_id(1) == 0)
    def _():
        kv_s[...] = _bf(jnp.dot(_bf(mem_ref[0]), wkv_ref[...], preferred_element_type=F32))

    q = q_ref[0] * (HEAD_DIM ** -0.5)
    lane = lax.broadcasted_iota(jnp.int32, (1, W), 1)
    kmat = kv_s[:, 0:W]
    vmat = kv_s[:, W:2 * W]
    out = jnp.zeros(q.shape, F32)
    for h in range(MEM_HEADS):
        sel = (lane >= h * HEAD_DIM) & (lane < (h + 1) * HEAD_DIM)
        s = lax.dot_general(_bf(jnp.where(sel, q, 0.0)), kmat, (((1,), (1,)), ((), ())),
                            preferred_element_type=F32)
        p = jnp.exp(s - jnp.max(s, axis=-1, keepdims=True))
        pv = jnp.dot(_bf(p), vmat, preferred_element_type=F32)
        out = jnp.where(sel, pv / jnp.sum(p, axis=-1, keepdims=True), out)
    o_ref[0] = out


def _mem_attention(q_m, mem, w_kv):
    B, S, W = q_m.shape
    _, M, D = mem.shape
    return pl.pallas_call(
        _mem_att_kernel,
        grid=(B, S // PROJ_ROWS),
        in_specs=[pl.BlockSpec((1, PROJ_ROWS, W), lambda b, i: (b, i, 0)),
                  pl.BlockSpec((1, M, D), lambda b, i: (b, 0, 0)),
                  pl.BlockSpec((D, 2 * W), lambda b, i: (0, 0))],
        out_specs=pl.BlockSpec((1, PROJ_ROWS, W), lambda b, i: (b, i, 0)),
        out_shape=jax.ShapeDtypeStruct((B, S, W), F32),
        scratch_shapes=[pltpu.VMEM((M, 2 * W), BF16)],
        compiler_params=pltpu.CompilerParams(dimension_semantics=("arbitrary", "arbitrary"),
                                             vmem_limit_bytes=VMEM_LIMIT),
        name="memory_attention",
    )(q_m, mem, _bf(w_kv))


def _out_proj_kernel(x_ref, yr_ref, ya_ref, ym_ref, g0_ref, b0_ref, w_ref, g1_ref, b1_ref, wrh_ref, wrl_ref,
                     br_ref, h1_ref, h1b_ref, gate_ref, expert_ref, *, alpha):
    h0 = _ln(x_ref[...], g0_ref[...], b0_ref[...])
    o = 0
    mix = None
    for ref in (yr_ref, ya_ref, ym_ref):
        n = ref.shape[-1]
        part = jnp.dot(_bf(ref[...]), w_ref[o:o + n, :], preferred_element_type=F32)
        mix = part if mix is None else mix + part
        o += n
    h1 = _ln(alpha * h0 + mix, g1_ref[...], b1_ref[...])
    h1_ref[...] = h1
    hi = _bf(h1)
    h1b_ref[...] = hi
    lo = _bf(h1 - hi.astype(F32))
    logits = (jnp.dot(hi, wrh_ref[...], preferred_element_type=F32)
              + jnp.dot(lo, wrh_ref[...], preferred_element_type=F32)
              + jnp.dot(hi, wrl_ref[...], preferred_element_type=F32)) + br_ref[...]
    lane = lax.broadcasted_iota(jnp.int32, logits.shape, 1).astype(F32)
    vals = logits
    top_v, top_i = [], []
    for _ in range(TOP_K):
        m = jnp.max(vals, axis=-1, keepdims=True)
        idx = jnp.min(jnp.where(vals == m, lane, float(LANES)), axis=-1, keepdims=True)
        top_v.append(m)
        top_i.append(idx)
        vals = jnp.where(lane == idx, -jnp.inf, vals)
    ex = [jnp.exp(v - top_v[0]) for v in top_v]
    den = ex[0] + ex[1] + ex[2] + ex[3]
    gates = jnp.zeros(logits.shape, F32)
    experts = jnp.zeros(logits.shape, F32)
    for k in range(TOP_K):
        gates = jnp.where(lane == k, ex[k] / den, gates)
        experts = jnp.where(lane == k, top_i[k], experts)
    gate_ref[...] = gates
    expert_ref[...] = experts.astype(jnp.int32)


def _out_proj(xt, y_rwkv, y_att, y_mem, ln0_g, ln0_b, w_out, ln1_g, ln1_b, w_router, b_router, alpha):
    T, D = xt.shape
    E = w_router.shape[1]
    wr = jnp.pad(w_router, ((0, 0), (0, LANES - E)))
    wr_hi = _bf(wr)
    wr_lo = _bf(wr - wr_hi.astype(F32))
    br = jnp.pad(b_router, (0, LANES - E), constant_values=NEG_INF).reshape(1, LANES)
    rows = lambda n: pl.BlockSpec((PROJ_ROWS, n), lambda i: (i, 0))
    full = lambda a: pl.BlockSpec(a.shape, lambda i: (0,) * a.ndim)
    vec = lambda t: t.reshape(1, D)
    consts = (vec(ln0_g), vec(ln0_b), _bf(w_out), vec(ln1_g), vec(ln1_b), wr_hi, wr_lo, br)
    return pl.pallas_call(
        functools.partial(_out_proj_kernel, alpha=alpha),
        grid=(T // PROJ_ROWS,),
        in_specs=[rows(D), rows(y_rwkv.shape[1]), rows(y_att.shape[1]), rows(y_mem.shape[1])]
                 + [full(a) for a in consts],
        out_specs=[rows(D), rows(D), rows(LANES), rows(LANES)],
        out_shape=[jax.ShapeDtypeStruct((T, D), F32), jax.ShapeDtypeStruct((T, D), BF16),
                   jax.ShapeDtypeStruct((T, LANES), F32), jax.ShapeDtypeStruct((T, LANES), jnp.int32)],
        compiler_params=pltpu.CompilerParams(dimension_semantics=("arbitrary",), vmem_limit_bytes=VMEM_LIMIT),
        name="out_proj_router",
    )(xt, y_rwkv, y_att, y_mem, *consts)


def _combine_kernel(h1_ref, y_ref, gate_ref, g_ref, b_ref, o_ref, *, alpha):
    acc = alpha * h1_ref[...]
    gates = gate_ref[...]
    for k in range(TOP_K):
        acc = acc + gates[:, k:k + 1] * y_ref[k].astype(F32)
    o_ref[...] = _ln(acc, g_ref[...], b_ref[...])


def _combine(h1, y_top, gates, ln_g, ln_b, alpha):
    T, D = h1.shape
    rows = lambda n: pl.BlockSpec((PROJ_ROWS, n), lambda i: (i, 0))
    full = lambda a: pl.BlockSpec(a.shape, lambda i: (0,) * a.ndim)
    consts = (ln_g.reshape(1, D), ln_b.reshape(1, D))
    return pl.pallas_call(
        functools.partial(_combine_kernel, alpha=alpha),
        grid=(T // PROJ_ROWS,),
        in_specs=[rows(D), pl.BlockSpec((TOP_K, PROJ_ROWS, D), lambda i: (0, i, 0)), rows(LANES)]
                 + [full(a) for a in consts],
        out_specs=rows(D),
        out_shape=jax.ShapeDtypeStruct((T, D), F32),
        compiler_params=pltpu.CompilerParams(dimension_semantics=("arbitrary",), vmem_limit_bytes=VMEM_LIMIT),
        name="combine_ln",
    )(h1, y_top, gates, *consts)


def kernel(x, mem, ln_in_g, ln_in_b, w_in, mu_shift, w0, w_up, a0, a_up, g_up, k_k, k_a, r_k, gn_g, gn_b, w_mem_kv, w_out, ln1_g, ln1_b, w_router, b_router, w_gate_up, b_gate_up, w_down, b_down, ln2_g, ln2_b):
    B, S, D = x.shape
    T = B * S
    depth = w_in.shape[0]
    assert depth == 1, "the layer norm feeding a layer is fused into its projection kernels"
    alpha = (2 * depth) ** 0.25
    l = 0
    xt = x.reshape(T, D)
    pad = RWKV_PAD_IN - RWKV_IN
    w_pad = _bf(jnp.concatenate([w_in[l][:, :RWKV_IN], jnp.zeros((D, pad), F32), w_in[l][:, RWKV_IN:]], axis=1))
    mu_pad = jnp.pad(mu_shift[l], (0, pad))
    p_rwkv, q_a, k_a_, v_a, q_m = _in_proj(xt, ln_in_g, ln_in_b, w_pad, mu_pad, S)
    seq = lambda t: t.reshape(B, S, t.shape[-1])
    y_rwkv = _rwkv_mixer(seq(p_rwkv), w0[l], w_up[l], a0[l], a_up[l], g_up[l], k_k[l], k_a[l], r_k[l],
                         gn_g[l], gn_b[l])
    y_att = _dilated_attention(seq(q_a), seq(k_a_), seq(v_a))
    y_mem = _mem_attention(seq(q_m), mem, w_mem_kv[l])
    flat = lambda t: t.reshape(T, t.shape[-1])
    h1, h1b, gates, experts = _out_proj(xt, flat(y_rwkv), flat(y_att), flat(y_mem), ln_in_g, ln_in_b, w_out[l],
                                        ln1_g[l], ln1_b[l], w_router[l], b_router[l], alpha)
    dest, row_token, block_e, n_used = _route(experts[:, :TOP_K], T)
    x_rows = jnp.take(h1b, row_token, axis=0, mode='clip')
    y_rows = _expert_blocks(block_e, n_used, x_rows, w_gate_up[l], b_gate_up[l], w_down[l], b_down[l])
    y_top = jnp.take(y_rows, dest.T.reshape(-1), axis=0, mode='clip').reshape(TOP_K, T, D)
    out = _combine(h1, y_top, gates, ln2_g[l], ln2_b[l], alpha)
    return out.reshape(B, S, D)
```

```python
import functools
import math

import jax
import jax.numpy as jnp
from jax import lax
from jax.experimental import pallas as pl
from jax.experimental.pallas import tpu as pltpu

F32 = jnp.float32
BF16 = jnp.bfloat16

HEAD_DIM = 64
RWKV_HEADS = 6
RWKV_WIDTH = RWKV_HEADS * HEAD_DIM
GATE_LORA = 64
DECAY_LORA = 32
ICLR_LORA = 32
LANES = 128
RWKV_CHUNK = 64
RWKV_BATCH = 2
RWKV_PAD_IN = 3 * RWKV_WIDTH + 2 * LANES
RWKV_GN_EPS = 64e-5
DECAY_SCALE = math.exp(-0.5)
VMEM_LIMIT = 56 * 1024 * 1024


def _bf(x):
    return x.astype(BF16)


def _dot(a, b):
    return jnp.dot(_bf(a), _bf(b), preferred_element_type=F32)


def _dot_nt(a, b):
    return lax.dot_general(_bf(a), _bf(b), (((1,), (1,)), ((), ())), preferred_element_type=F32)


def _dot_tn(a, b):
    return lax.dot_general(_bf(a), _bf(b), (((0,), (0,)), ((), ())), preferred_element_type=F32)


def _split3(x):
    hi = _bf(x)
    r1 = x - hi.astype(F32)
    mid = _bf(r1)
    lo = _bf(r1 - mid.astype(F32))
    return hi, mid, lo


def _dot_exact_lhs(a_bf, x, terms=3):
    parts = _split3(x)[:terms]
    acc = jnp.dot(a_bf, parts[0], preferred_element_type=F32)
    for p in parts[1:]:
        acc = acc + jnp.dot(a_bf, p, preferred_element_type=F32)
    return acc


def _dot_exact_rhs(x, b_bf, terms=2):
    parts = _split3(x)[:terms]
    acc = jnp.dot(parts[0], b_bf, preferred_element_type=F32)
    for p in parts[1:]:
        acc = acc + jnp.dot(p, b_bf, preferred_element_type=F32)
    return acc


def _sigmoid(x):
    return 1.0 / (1.0 + jnp.exp(-x))


def _rwkv_kernel(pf_ref, pb_ref, wup_ref, aup_ref, gup_ref, w0_ref, a0_ref, kk_ref, ka_ref, rk_ref,
                 gng_ref, gnb_ref, hsum_ref, o_ref, state_ref, *, seq, chunk):
    C, N, H, W = chunk, HEAD_DIM, RWKV_HEADS, RWKV_WIDTH
    nc = seq // C
    c = pl.program_id(1)

    @pl.when(c == 0)
    def _():
        state_ref[...] = jnp.zeros_like(state_ref)

    hsum = hsum_ref[...]
    row = lax.broadcasted_iota(jnp.int32, (C, C), 0)
    col = lax.broadcasted_iota(jnp.int32, (C, C), 1)

    chunk_of = (c, nc - 1 - c)
    groups = [(bb, d) for d in range(2) for bb in range(RWKV_BATCH)]
    G = len(groups)
    rows_of = {g: slice(i * C, (i + 1) * C) for i, g in enumerate(groups)}
    stack = lambda parts: jnp.concatenate(parts, axis=0)

    def prepare(bb, d):
        ps = (pf_ref if d == 0 else pb_ref)[bb]
        r = ps[:, 0:W]
        k = ps[:, W:2 * W]
        v = ps[:, 2 * W:3 * W]
        x1 = ps[:, 3 * W:3 * W + LANES]
        x2 = ps[:, 3 * W + LANES:3 * W + 2 * LANES]
        a_both = [_sigmoid(a0_ref[e:e + 1, :] + _dot(x2, aup_ref[e])) for e in range(2)]
        w_logit = w0_ref[d:d + 1, :] + _dot(jnp.tanh(x1), wup_ref[d])
        logw = -DECAY_SCALE * _sigmoid(w_logit)
        kkr = k * kk_ref[...]
        ss = _dot_exact_rhs(kkr * kkr, hsum)
        kk = kkr / jnp.maximum(jnp.sqrt(ss), 1e-12)
        kdir_both = [k * (1.0 + (a - 1.0) * ka_ref[...]) for a in a_both]
        kdir = kdir_both[d]
        b = kk * a_both[d]
        incl = (row >= col) if d == 0 else (row <= col)
        tri = jnp.where(incl, 1.0, 0.0).astype(BF16)
        lin = _dot_exact_lhs(tri, logw)
        tot = jnp.sum(logw, axis=0, keepdims=True)
        e_neg = jnp.exp(-lin)
        e_tot = jnp.exp(tot)
        kh = kdir * e_neg
        bh = b * e_neg
        return dict(r=r, v=v, x1=x1, ksum=kdir_both[0] + kdir_both[1], rt=r * jnp.exp(lin),
                    at=kk * jnp.exp(lin - logw), kh=kh, bh=bh, kc=kh * e_tot, bc=bh * e_tot, e_tot=e_tot)

    pre = {g: prepare(*g) for g in groups}
    chains = [(bb, d, h) for bb, d in groups for h in range(H)]
    assert 2 * N == LANES and C == N
    r2 = lax.broadcasted_iota(jnp.int32, (2 * C, LANES), 0)
    l2 = lax.broadcasted_iota(jnp.int32, (2 * C, LANES), 1)
    t_row = jnp.where(r2 >= C, r2 - C, r2)
    t_col = jnp.where(l2 >= N, l2 - N, l2)
    before = (t_row > t_col, t_row < t_col)
    diag_r = jnp.logical_and(r2 >= C, t_row == t_col)
    tile_mask = [jnp.logical_or(before[d], diag_r) for d in range(2)]
    left = lax.broadcasted_iota(jnp.int32, (C, LANES), 1) < N
    zeros_cn = jnp.zeros((C, N), F32)
    zeros_cl = jnp.zeros((C, LANES), F32)

    s0, tile0, ws = {}, {}, {}
    for bb, d, h in chains:
        sl = slice(h * N, (h + 1) * N)
        q = pre[bb, d]
        lhs2 = jnp.concatenate([q['at'][:, sl], q['rt'][:, sl]], axis=0)
        s0[bb, d, h] = state_ref[bb, d, h]
        rhs3 = jnp.concatenate([q['kh'][:, sl], q['bh'][:, sl], s0[bb, d, h]], axis=0)
        g = _dot_nt(lhs2, rhs3)
        tile0[bb, d, h] = jnp.where(tile_mask[d], g[:, 0:LANES], 0.0)
        ws[bb, d, h] = g[:, LANES:LANES + N]
    av = {}
    for bb, d, h in chains:
        sl = slice(h * N, (h + 1) * N)
        lhs = jnp.where(l2 < N, tile0[bb, d, h], 0.0)
        av[bb, d, h] = _dot(lhs, jnp.concatenate([pre[bb, d]['v'][:, sl], zeros_cn], axis=0))
    z = {}
    for ch in chains:
        u0 = jnp.concatenate([ws[ch][0:C] + av[ch][0:C], zeros_cn], axis=1)
        z[ch] = jnp.where(left, u0, -tile0[ch][0:C])
    for _ in range(int(math.log2(C))):
        for ch in chains:
            prod = _dot(jnp.where(left, 0.0, z[ch]), jnp.concatenate([zeros_cl, z[ch]], axis=0))
            z[ch] = jnp.where(left, z[ch] + prod, prod)
    ys = {}
    for bb, d, h in chains:
        ch = (bb, d, h)
        sl = slice(h * N, (h + 1) * N)
        q = pre[bb, d]
        u = z[ch][:, 0:N]
        rb = jnp.where(left, 0.0, tile0[ch][C:2 * C])
        ys[ch] = ws[ch][C:2 * C] + av[ch][C:2 * C] - _dot(rb, jnp.concatenate([zeros_cn, u], axis=0))
        upd = _dot_tn(jnp.concatenate([q['v'][:, sl], u], axis=0),
                      jnp.concatenate([q['kc'][:, sl], -q['bc'][:, sl]], axis=0))
        state_ref[bb, d, h] = s0[ch] * q['e_tot'][:, sl] + upd

    y = {g: jnp.concatenate([ys[g + (h,)] for h in range(H)], axis=1) for g in groups}
    base = [pl.multiple_of(chunk_of[d] * C, C) for d in range(2)]

    @pl.when(c < nc // 2)
    def _():
        for bb, d in groups:
            o_ref[bb, pl.ds(base[d], C), :] = y[bb, d]

    @pl.when(c >= nc // 2)
    def _():
        inv_n = 1.0 / N
        yt = stack([o_ref[bb, pl.ds(base[d], C), :] + y[bb, d] for bb, d in groups])
        bonus_in = stack([pre[g]['r'] * rk_ref[...] * pre[g]['ksum'] for g in groups])
        sums = _dot_exact_rhs(stack([yt, bonus_in]), hsum)
        yc = yt - sums[0:G * C] * inv_n
        coef = sums[G * C:2 * G * C]
        var = _dot_exact_rhs(yc * yc, hsum) * inv_n
        yn = yc * lax.rsqrt(var + RWKV_GN_EPS) * gng_ref[...] + gnb_ref[...]
        gate = _dot(_sigmoid(stack([pre[g]['x1'] for g in groups])), gup_ref[...])
        v_all = stack([pre[g]['v'] for g in groups])
        out = (yn + coef * v_all) * gate
        for bb, d in groups:
            o_ref[bb, pl.ds(base[d], C), :] = out[rows_of[bb, d]]


def _rwkv_mixer(p_shift, w0, w_up, a0, a_up, g_up, k_k, k_a, r_k, gn_g, gn_b):
    B, S, PW = p_shift.shape
    W = RWKV_WIDTH
    C = RWKV_CHUNK
    BB = RWKV_BATCH
    nc = S // C
    assert B % BB == 0 and nc % 2 == 0
    wup = jnp.zeros((2, LANES, W), F32)
    for d in range(2):
        lo = GATE_LORA + d * DECAY_LORA
        wup = wup.at[d, lo:lo + DECAY_LORA].set(w_up[d])
    aup = jnp.zeros((2, LANES, W), F32)
    for d in range(2):
        aup = aup.at[d, d * ICLR_LORA:(d + 1) * ICLR_LORA].set(a_up[d])
    gup = jnp.zeros((LANES, W), F32).at[:GATE_LORA].set(g_up)
    head = jnp.arange(W) // HEAD_DIM
    hsum = (head[:, None] == head[None, :]).astype(BF16)
    row = lambda t: t.reshape(1, W)
    const2 = lambda shape: pl.BlockSpec(shape, lambda b, c: (0,) * len(shape))
    return pl.pallas_call(
        functools.partial(_rwkv_kernel, seq=S, chunk=C),
        grid=(B // BB, nc),
        in_specs=[pl.BlockSpec((BB, C, PW), lambda b, c: (b, c, 0)),
                  pl.BlockSpec((BB, C, PW), lambda b, c: (b, nc - 1 - c, 0)),
                  const2((2, LANES, W)), const2((2, LANES, W)), const2((LANES, W)),
                  const2((2, W)), const2((2, W)), const2((1, W)), const2((1, W)), const2((1, W)),
                  const2((1, W)), const2((1, W)), const2((W, W))],
        out_specs=pl.BlockSpec((BB, S, W), lambda b, c: (b, 0, 0)),
        out_shape=jax.ShapeDtypeStruct((B, S, W), F32),
        scratch_shapes=[pltpu.VMEM((BB, 2, RWKV_HEADS, HEAD_DIM, HEAD_DIM), F32)],
        compiler_params=pltpu.CompilerParams(dimension_semantics=("arbitrary", "arbitrary"),
                                             vmem_limit_bytes=VMEM_LIMIT),
        name="rwkv7_scan",
    )(p_shift, p_shift, _bf(wup), _bf(aup), _bf(gup), w0, a0, row(k_k), row(k_a),
      r_k.reshape(1, W), row(gn_g), row(gn_b), hsum)


ATT_HEADS = 6
ATT_WIDTH = ATT_HEADS * HEAD_DIM
DILATED_BRANCHES = ((128, 1), (512, 4), (2048, 16))
ATT_BLOCK = 64
NEG_INF = -1e30
ATT_UNROLL = 16


def _rows(start, size, stride):
    return pl.ds(start, size) if stride == 1 else pl.ds(start, size, stride=stride)


def _att_kernel(slope_ref, q_ref, k_ref, v_ref, out_ref, o_s, lse_s, *, seq):
    S, Q = seq, ATT_BLOCK
    left = lax.broadcasted_iota(jnp.int32, (1, LANES), 1) < HEAD_DIM
    qi = lax.broadcasted_iota(jnp.int32, (Q, 3 * Q), 0)
    kj = lax.broadcasted_iota(jnp.int32, (Q, 3 * Q), 1)
    dist = jnp.abs(kj - Q - qi).astype(F32)
    band = dist <= Q
    kj_row = lax.broadcasted_iota(jnp.int32, (1, 3 * Q), 1)
    ones_v = jnp.ones((3 * Q, LANES), BF16)
    for bi, (window, dil) in enumerate(DILATED_BRANCHES):
        assert window // (2 * dil) == Q and S % (Q * dil) == 0
        nb = S // (Q * dil)
        bias = jnp.concatenate([jnp.where(band, (-dil * slope_ref[0, hh:hh + 1, 0:1]) * dist, NEG_INF)
                                for hh in range(2)], axis=0)

        def body(it, carry, dil=dil, nb=nb, bias=bias, bi=bi):
            blocks = []
            whole_classes = ATT_UNROLL % nb == 0
            if whole_classes:
                starts = [(it * ATT_UNROLL + uu) // nb + dil * Q * (uu % nb) for uu in range(ATT_UNROLL)]
                k_own = [_bf(k_ref[0, _rows(s, Q, dil), :]) for s in starts]
                v_own = [_bf(v_ref[0, _rows(s, Q, dil), :]) for s in starts]
            for uu in range(ATT_UNROLL):
                if whole_classes:
                    n = uu % nb
                    start = starts[uu]
                    lo = uu - 1 if n > 0 else uu
                    hi = uu + 1 if n < nb - 1 else uu
                    kw = jnp.concatenate([k_own[lo], k_own[uu], k_own[hi]], axis=0)
                    vw = jnp.concatenate([v_own[lo], v_own[uu], v_own[hi]], axis=0)
                else:
                    i = it * ATT_UNROLL + uu
                    n = i % nb
                    start = pl.multiple_of(i * Q, Q) if dil == 1 else i // nb + dil * Q * n
                    s_prev = start - jnp.where(n > 0, dil * Q, 0)
                    s_next = start + jnp.where(n < nb - 1, dil * Q, 0)
                    if dil == 1:
                        s_prev, s_next = pl.multiple_of(s_prev, Q), pl.multiple_of(s_next, Q)
                    kw = _bf(jnp.concatenate([k_ref[0, _rows(s, Q, dil), :] for s in (s_prev, start, s_next)], axis=0))
                    vw = _bf(jnp.concatenate([v_ref[0, _rows(s, Q, dil), :] for s in (s_prev, start, s_next)], axis=0))
                q = q_ref[0, _rows(start, Q, dil), :] * (HEAD_DIM ** -0.5)
                pen_prev = jnp.where(n == 0, NEG_INF, 0.0)
                pen_next = jnp.where(n == nb - 1, NEG_INF, 0.0)
                edge = jnp.where(kj_row < Q, pen_prev, jnp.where(kj_row >= 2 * Q, pen_next, 0.0))
                blocks.append((start, q, kw, jnp.concatenate([vw, ones_v], axis=1), edge))
            scores = [_dot_nt(jnp.concatenate([jnp.where(left, q, 0.0), jnp.where(left, 0.0, q)], axis=0), kw)
                      + bias + edge for (_, q, kw, _, edge) in blocks]
            mx = [jnp.max(s, axis=-1, keepdims=True) for s in scores]
            pr = [jnp.exp(s - m) for s, m in zip(scores, mx)]
            pv = [jnp.dot(_bf(p), blk[3], preferred_element_type=F32) for p, blk in zip(pr, blocks)]
            for uu, blk in enumerate(blocks):
                rows = _rows(blk[0], Q, dil)
                den = jnp.where(left, pv[uu][0:Q, LANES:2 * LANES], pv[uu][Q:2 * Q, LANES:2 * LANES])
                o_s[bi, rows, :] = jnp.where(left, pv[uu][0:Q, 0:LANES], pv[uu][Q:2 * Q, 0:LANES]) / den
                lse_s[bi, rows, :] = jnp.where(left, mx[uu][0:Q], mx[uu][Q:2 * Q]) + jnp.log(den)
            return carry

        lax.fori_loop(0, S // Q // ATT_UNROLL, body, 0)

    def merge(j, carry):
        rows = pl.ds(pl.multiple_of(j * Q, Q), Q)
        ms = [lse_s[b, rows, :] for b in range(3)]
        top = jnp.maximum(jnp.maximum(ms[0], ms[1]), ms[2])
        ws = [jnp.exp(m - top) for m in ms]
        num = ws[0] * o_s[0, rows, :] + ws[1] * o_s[1, rows, :] + ws[2] * o_s[2, rows, :]
        out_ref[0, rows, :] = num / (ws[0] + ws[1] + ws[2])
        return carry

    lax.fori_loop(0, S // Q, merge, 0)


def _dilated_attention(q, k, v):
    B, S, W = q.shape
    pairs = W // LANES
    slopes = jnp.exp2(-8.0 * jnp.arange(1, ATT_HEADS + 1, dtype=F32) / ATT_HEADS).reshape(pairs, 2, 1)
    slopes = jnp.broadcast_to(jnp.pad(slopes, ((0, 0), (0, 6), (0, 0))), (pairs, 8, LANES))
    spec = pl.BlockSpec((1, S, LANES), lambda b, p: (b, 0, p))
    return pl.pallas_call(
        functools.partial(_att_kernel, seq=S),
        grid=(B, pairs),
        in_specs=[pl.BlockSpec((1, 8, LANES), lambda b, p: (p, 0, 0)), spec, spec, spec],
        out_specs=spec,
        out_shape=jax.ShapeDtypeStruct((B, S, W), F32),
        scratch_shapes=[pltpu.VMEM((3, S, LANES), F32)] * 2,
        compiler_params=pltpu.CompilerParams(dimension_semantics=("arbitrary", "arbitrary"),
                                             vmem_limit_bytes=VMEM_LIMIT),
        name="dilated_attention",
    )(slopes, q, k, v)


N_EXPERTS = 32
TOP_K = 4
SWIGLU_LIMIT = 7.0
SWIGLU_ALPHA = 1.702
MOE_ROWS = 512


CAST_ROWS = 128


def _expert_kernel(be_ref, nb_ref, x_ref, wgu_ref, bgu_ref, wd_ref, bd_ref, o_ref, wgu_s, wd_s):
    i = pl.program_id(0)
    F = wd_ref.shape[1]
    used = i < nb_ref[0]

    @pl.when(used & ((i == 0) | (be_ref[i] != be_ref[jnp.maximum(i - 1, 0)])))
    def _():
        def cast(src, dst):
            def step(j, carry):
                rows = pl.ds(pl.multiple_of(j * CAST_ROWS, CAST_ROWS), CAST_ROWS)
                dst[rows, :] = _bf(src[0, rows, :])
                return carry
            lax.fori_loop(0, src.shape[1] // CAST_ROWS, step, 0)
        cast(wgu_ref, wgu_s)
        cast(wd_ref, wd_s)

    @pl.when(used)
    def _():
        gu = jnp.dot(x_ref[...], wgu_s[...], preferred_element_type=F32) + bgu_ref[0]
        gate = jnp.minimum(gu[:, :F], SWIGLU_LIMIT)
        up = jnp.clip(gu[:, F:], -SWIGLU_LIMIT, SWIGLU_LIMIT)
        act = (up + 1.0) * gate * _sigmoid(SWIGLU_ALPHA * gate)
        o_ref[...] = (jnp.dot(_bf(act), wd_s[...], preferred_element_type=F32) + bd_ref[0]).astype(o_ref.dtype)

    @pl.when(jnp.logical_not(used))
    def _():
        o_ref[...] = jnp.zeros_like(o_ref)


def _expert_blocks(block_e, n_used, x_rows, w_gate_up, b_gate_up, w_down, b_down):
    n_rows, D = x_rows.shape
    E, _, F2 = w_gate_up.shape
    F = F2 // 2
    n_blocks = n_rows // MOE_ROWS
    return pl.pallas_call(
        _expert_kernel,
        grid_spec=pltpu.PrefetchScalarGridSpec(
            num_scalar_prefetch=2,
            grid=(n_blocks,),
            in_specs=[pl.BlockSpec((MOE_ROWS, D), lambda i, be, nb: (i, 0)),
                      pl.BlockSpec((1, D, F2), lambda i, be, nb: (be[i], 0, 0)),
                      pl.BlockSpec((1, 1, F2), lambda i, be, nb: (be[i], 0, 0)),
                      pl.BlockSpec((1, F, D), lambda i, be, nb: (be[i], 0, 0)),
                      pl.BlockSpec((1, 1, D), lambda i, be, nb: (be[i], 0, 0))],
            out_specs=pl.BlockSpec((MOE_ROWS, D), lambda i, be, nb: (i, 0)),
            scratch_shapes=[pltpu.VMEM((D, F2), BF16), pltpu.VMEM((F, D), BF16)]),
        out_shape=jax.ShapeDtypeStruct((n_rows, D), BF16),
        compiler_params=pltpu.CompilerParams(dimension_semantics=("arbitrary",),
                                             vmem_limit_bytes=VMEM_LIMIT),
        name="moe_experts",
    )(block_e, n_used, x_rows, w_gate_up, b_gate_up.reshape(E, 1, F2), w_down, b_down.reshape(E, 1, D))


def _route(top_e, n_tokens):
    T = n_tokens
    n_rows = T * TOP_K + N_EXPERTS * MOE_ROWS
    n_blocks = n_rows // MOE_ROWS
    experts = jnp.arange(N_EXPERTS, dtype=jnp.int32)
    onehot = jnp.sum((top_e[:, :, None] == experts).astype(jnp.int32), axis=1)
    incl = jnp.cumsum(onehot, axis=0)
    counts = incl[-1]
    padded = (counts + MOE_ROWS - 1) // MOE_ROWS * MOE_ROWS
    starts = jnp.cumsum(counts) - counts
    pends = jnp.cumsum(padded)
    pstarts = pends - padded
    rank = jnp.take_along_axis(incl - onehot, top_e, axis=1)
    dest = pstarts[top_e] + rank
    order = jnp.argsort(top_e.reshape(-1), stable=True).astype(jnp.int32)
    block_start = jnp.arange(n_blocks, dtype=jnp.int32) * MOE_ROWS
    block_e = jnp.minimum(jnp.sum((pends[None, :] <= block_start[:, None]).astype(jnp.int32), axis=1),
                          N_EXPERTS - 1)
    row = jnp.arange(n_rows, dtype=jnp.int32)
    within = row - jnp.repeat(pstarts[block_e], MOE_ROWS)
    src = jnp.clip(within + jnp.repeat(starts[block_e], MOE_ROWS), 0, T * TOP_K - 1)
    row_token = jnp.where(within < jnp.repeat(counts[block_e], MOE_ROWS),
                          jnp.take(order, src, mode='clip') // TOP_K, row % T)
    n_used = (pends[-1:] // MOE_ROWS).astype(jnp.int32)
    return dest.astype(jnp.int32), row_token.astype(jnp.int32), block_e.astype(jnp.int32), n_used


LN_EPS = 1e-5
MEM_HEADS = 4
MEM_WIDTH = MEM_HEADS * HEAD_DIM
RWKV_IN = 3 * RWKV_WIDTH + GATE_LORA + 2 * DECAY_LORA + 2 * ICLR_LORA
PROJ_ROWS = 512
BATCH_GROUPS = 1


def _ln(x, g, b):
    mu = jnp.mean(x, -1, keepdims=True)
    xc = x - mu
    var = jnp.mean(xc * xc, -1, keepdims=True)
    return xc * lax.rsqrt(var + LN_EPS) * g + b


HALO = 8


def _in_proj_kernel(x_ref, xp_ref, xn_ref, g_ref, b_ref, w_ref, mu_ref, rw_ref, q_ref, k_ref, v_ref, qm_ref, *, seq):
    R = PROJ_ROWS
    i = pl.program_id(0)
    xs = jnp.concatenate([x_ref[...], xp_ref[...], xn_ref[...]], axis=0)
    hb = _bf(_ln(xs, g_ref[...], b_ref[...]))
    n_rw = rw_ref.shape[-1]
    p = jnp.dot(hb, w_ref[:, 0:n_rw], preferred_element_type=F32)
    cur = p[0:R]
    tiles_per_seq = seq // R
    first = (i % tiles_per_seq) == 0
    last = (i % tiles_per_seq) == tiles_per_seq - 1
    prev_row = jnp.where(first, 0.0, p[R + HALO - 1:R + HALO])
    next_row = jnp.where(last, 0.0, p[R + HALO:R + HALO + 1])
    rowi = lax.broadcasted_iota(jnp.int32, (R, 1), 0)
    prev = jnp.where(rowi == 0, prev_row, pltpu.roll(cur, 1, 0))
    nxt = jnp.where(rowi == R - 1, next_row, pltpu.roll(cur, R - 1, 0))
    rw_ref[...] = cur + mu_ref[...] * (0.5 * (prev + nxt) - cur)
    o = n_rw
    hb_main = hb[0:R]
    for ref in (q_ref, k_ref, v_ref, qm_ref):
        n = ref.shape[-1]
        ref[...] = jnp.dot(hb_main, w_ref[:, o:o + n], preferred_element_type=F32)
        o += n


def _in_proj(xt, ln_g, ln_b, w_pad, mu_pad, seq):
    T, D = xt.shape
    widths = (RWKV_PAD_IN, ATT_WIDTH, ATT_WIDTH, ATT_WIDTH, MEM_WIDTH)
    assert sum(widths) == w_pad.shape[1] and seq % PROJ_ROWS == 0
    per_tile = PROJ_ROWS // HALO
    last_halo = T // HALO - 1
    rows = lambda n: pl.BlockSpec((PROJ_ROWS, n), lambda i: (i, 0))
    full = lambda a: pl.BlockSpec(a.shape, lambda i: (0,) * a.ndim)
    consts = (ln_g.reshape(1, D), ln_b.reshape(1, D), w_pad, mu_pad.reshape(1, RWKV_PAD_IN))
    return pl.pallas_call(
        functools.partial(_in_proj_kernel, seq=seq),
        grid=(T // PROJ_ROWS,),
        in_specs=[rows(D),
                  pl.BlockSpec((HALO, D), lambda i: (jnp.maximum(i * per_tile - 1, 0), 0)),
                  pl.BlockSpec((HALO, D), lambda i: (jnp.minimum((i + 1) * per_tile, last_halo), 0))]
                 + [full(a) for a in consts],
        out_specs=[rows(n) for n in widths],
        out_shape=[jax.ShapeDtypeStruct((T, n), F32) for n in widths],
        compiler_params=pltpu.CompilerParams(dimension_semantics=("arbitrary",), vmem_limit_bytes=VMEM_LIMIT),
        name="in_proj",
    )(xt, xt, xt, *consts)


def _mem_att_kernel(q_ref, mem_ref, wkv_ref, o_ref, kv_s):
    W = MEM_WIDTH

    @pl.when(pl.program_id(1) == 0)
    def _():
        kv_s[...] = _bf(jnp.dot(_bf(mem_ref[0]), wkv_ref[...], preferred_element_type=F32))

    q = q_ref[0] * (HEAD_DIM ** -0.5)
    lane = lax.broadcasted_iota(jnp.int32, (1, W), 1)
    kmat = kv_s[:, 0:W]
    vmat = kv_s[:, W:2 * W]
    out = jnp.zeros(q.shape, F32)
    for h in range(MEM_HEADS):
        sel = (lane >= h * HEAD_DIM) & (lane < (h + 1) * HEAD_DIM)
        s = lax.dot_general(_bf(jnp.where(sel, q, 0.0)), kmat, (((1,), (1,)), ((), ())),
                            preferred_element_type=F32)
        p = jnp.exp(s - jnp.max(s, axis=-1, keepdims=True))
        pv = jnp.dot(_bf(p), vmat, preferred_element_type=F32)
        out = jnp.where(sel, pv / jnp.sum(p, axis=-1, keepdims=True), out)
    o_ref[0] = out


def _mem_attention(q_m, mem, w_kv):
    B, S, W = q_m.shape
    _, M, D = mem.shape
    return pl.pallas_call(
        _mem_att_kernel,
        grid=(B, S // PROJ_ROWS),
        in_specs=[pl.BlockSpec((1, PROJ_ROWS, W), lambda b, i: (b, i, 0)),
                  pl.BlockSpec((1, M, D), lambda b, i: (b, 0, 0)),
                  pl.BlockSpec((D, 2 * W), lambda b, i: (0, 0))],
        out_specs=pl.BlockSpec((1, PROJ_ROWS, W), lambda b, i: (b, i, 0)),
        out_shape=jax.ShapeDtypeStruct((B, S, W), F32),
        scratch_shapes=[pltpu.VMEM((M, 2 * W), BF16)],
        compiler_params=pltpu.CompilerParams(dimension_semantics=("arbitrary", "arbitrary"),
                                             vmem_limit_bytes=VMEM_LIMIT),
        name="memory_attention",
    )(q_m, mem, _bf(w_kv))


def _out_proj_kernel(x_ref, yr_ref, ya_ref, ym_ref, g0_ref, b0_ref, w_ref, g1_ref, b1_ref, wrh_ref, wrl_ref,
                     br_ref, h1_ref, h1b_ref, gate_ref, expert_ref, *, alpha):
    h0 = _ln(x_ref[...], g0_ref[...], b0_ref[...])
    o = 0
    mix = None
    for ref in (yr_ref, ya_ref, ym_ref):
        n = ref.shape[-1]
        part = jnp.dot(_bf(ref[...]), w_ref[o:o + n, :], preferred_element_type=F32)
        mix = part if mix is None else mix + part
        o += n
    h1 = _ln(alpha * h0 + mix, g1_ref[...], b1_ref[...])
    h1_ref[...] = h1
    hi = _bf(h1)
    h1b_ref[...] = hi
    lo = _bf(h1 - hi.astype(F32))
    logits = (jnp.dot(hi, wrh_ref[...], preferred_element_type=F32)
              + jnp.dot(lo, wrh_ref[...], preferred_element_type=F32)
              + jnp.dot(hi, wrl_ref[...], preferred_element_type=F32)) + br_ref[...]
    lane = lax.broadcasted_iota(jnp.int32, logits.shape, 1).astype(F32)
    vals = logits
    top_v, top_i = [], []
    for _ in range(TOP_K):
        m = jnp.max(vals, axis=-1, keepdims=True)
        idx = jnp.min(jnp.where(vals == m, lane, float(LANES)), axis=-1, keepdims=True)
        top_v.append(m)
        top_i.append(idx)
        vals = jnp.where(lane == idx, -jnp.inf, vals)
    ex = [jnp.exp(v - top_v[0]) for v in top_v]
    den = ex[0] + ex[1] + ex[2] + ex[3]
    gates = jnp.zeros(logits.shape, F32)
    experts = jnp.zeros(logits.shape, F32)
    for k in range(TOP_K):
        gates = jnp.where(lane == k, ex[k] / den, gates)
        experts = jnp.where(lane == k, top_i[k], experts)
    gate_ref[...] = gates
    expert_ref[...] = experts.astype(jnp.int32)


def _out_proj(xt, y_rwkv, y_att, y_mem, ln0_g, ln0_b, w_out, ln1_g, ln1_b, w_router, b_router, alpha):
    T, D = xt.shape
    E = w_router.shape[1]
    wr = jnp.pad(w_router, ((0, 0), (0, LANES - E)))
    wr_hi = _bf(wr)
    wr_lo = _bf(wr - wr_hi.astype(F32))
    br = jnp.pad(b_router, (0, LANES - E), constant_values=NEG_INF).reshape(1, LANES)
    rows = lambda n: pl.BlockSpec((PROJ_ROWS, n), lambda i: (i, 0))
    full = lambda a: pl.BlockSpec(a.shape, lambda i: (0,) * a.ndim)
    vec = lambda t: t.reshape(1, D)
    consts = (vec(ln0_g), vec(ln0_b), _bf(w_out), vec(ln1_g), vec(ln1_b), wr_hi, wr_lo, br)
    return pl.pallas_call(
        functools.partial(_out_proj_kernel, alpha=alpha),
        grid=(T // PROJ_ROWS,),
        in_specs=[rows(D), rows(y_rwkv.shape[1]), rows(y_att.shape[1]), rows(y_mem.shape[1])]
                 + [full(a) for a in consts],
        out_specs=[rows(D), rows(D), rows(LANES), rows(LANES)],
        out_shape=[jax.ShapeDtypeStruct((T, D), F32), jax.ShapeDtypeStruct((T, D), BF16),
                   jax.ShapeDtypeStruct((T, LANES), F32), jax.ShapeDtypeStruct((T, LANES), jnp.int32)],
        compiler_params=pltpu.CompilerParams(dimension_semantics=("arbitrary",), vmem_limit_bytes=VMEM_LIMIT),
        name="out_proj_router",
    )(xt, y_rwkv, y_att, y_mem, *consts)


def _combine_kernel(h1_ref, y_ref, gate_ref, g_ref, b_ref, o_ref, *, alpha):
    acc = alpha * h1_ref[...]
    gates = gate_ref[...]
    for k in range(TOP_K):
        acc = acc + gates[:, k:k + 1] * y_ref[k].astype(F32)
    o_ref[...] = _ln(acc, g_ref[...], b_ref[...])


def _combine(h1, y_top, gates, ln_g, ln_b, alpha):
    T, D = h1.shape
    rows = lambda n: pl.BlockSpec((PROJ_ROWS, n), lambda i: (i, 0))
    full = lambda a: pl.BlockSpec(a.shape, lambda i: (0,) * a.ndim)
    consts = (ln_g.reshape(1, D), ln_b.reshape(1, D))
    return pl.pallas_call(
        functools.partial(_combine_kernel, alpha=alpha),
        grid=(T // PROJ_ROWS,),
        in_specs=[rows(D), pl.BlockSpec((TOP_K, PROJ_ROWS, D), lambda i: (0, i, 0)), rows(LANES)]
                 + [full(a) for a in consts],
        out_specs=rows(D),
        out_shape=jax.ShapeDtypeStruct((T, D), F32),
        compiler_params=pltpu.CompilerParams(dimension_semantics=("arbitrary",), vmem_limit_bytes=VMEM_LIMIT),
        name="combine_ln",
    )(h1, y_top, gates, *consts)


def kernel(x, mem, ln_in_g, ln_in_b, w_in, mu_shift, w0, w_up, a0, a_up, g_up, k_k, k_a, r_k, gn_g, gn_b, w_mem_kv, w_out, ln1_g, ln1_b, w_router, b_router, w_gate_up, b_gate_up, w_down, b_down, ln2_g, ln2_b):
    B, S, D = x.shape
    depth = w_in.shape[0]
    assert depth == 1, "the layer norm feeding a layer is fused into its projection kernels"
    assert B % BATCH_GROUPS == 0
    alpha = (2 * depth) ** 0.25
    l = 0
    pad = RWKV_PAD_IN - RWKV_IN
    w_pad = _bf(jnp.concatenate([w_in[l][:, :RWKV_IN], jnp.zeros((D, pad), F32), w_in[l][:, RWKV_IN:]], axis=1))
    mu_pad = jnp.pad(mu_shift[l], (0, pad))

    def layer(xg, memg):
        Bg = xg.shape[0]
        T = Bg * S
        xt = xg.reshape(T, D)
        p_rwkv, q_a, k_a_, v_a, q_m = _in_proj(xt, ln_in_g, ln_in_b, w_pad, mu_pad, S)
        seq = lambda t: t.reshape(Bg, S, t.shape[-1])
        y_rwkv = _rwkv_mixer(seq(p_rwkv), w0[l], w_up[l], a0[l], a_up[l], g_up[l], k_k[l], k_a[l], r_k[l],
                             gn_g[l], gn_b[l])
        y_att = _dilated_attention(seq(q_a), seq(k_a_), seq(v_a))
        y_mem = _mem_attention(seq(q_m), memg, w_mem_kv[l])
        flat = lambda t: t.reshape(T, t.shape[-1])
        h1, h1b, gates, experts = _out_proj(xt, flat(y_rwkv), flat(y_att), flat(y_mem), ln_in_g, ln_in_b, w_out[l],
                                            ln1_g[l], ln1_b[l], w_router[l], b_router[l], alpha)
        dest, row_token, block_e, n_used = _route(experts[:, :TOP_K], T)
        x_rows = jnp.take(h1b, row_token, axis=0, mode='clip')
        y_rows = _expert_blocks(block_e, n_used, x_rows, w_gate_up[l], b_gate_up[l], w_down[l], b_down[l])
        y_top = jnp.take(y_rows, dest.T.reshape(-1), axis=0, mode='clip').reshape(TOP_K, T, D)
        return _combine(h1, y_top, gates, ln2_g[l], ln2_b[l], alpha).reshape(Bg, S, D)

    Bg = B // BATCH_GROUPS
    outs = [layer(x[g * Bg:(g + 1) * Bg], mem[g * Bg:(g + 1) * Bg]) for g in range(BATCH_GROUPS)]
    return outs[0] if BATCH_GROUPS == 1 else jnp.concatenate(outs, axis=0)
```

```python
import functools
import math

import jax
import jax.numpy as jnp
from jax import lax
from jax.experimental import pallas as pl
from jax.experimental.pallas import tpu as pltpu

F32 = jnp.float32
BF16 = jnp.bfloat16

HEAD_DIM = 64
RWKV_HEADS = 6
RWKV_WIDTH = RWKV_HEADS * HEAD_DIM
GATE_LORA = 64
DECAY_LORA = 32
ICLR_LORA = 32
LANES = 128
RWKV_CHUNK = 64
RWKV_BATCH = 4
RWKV_PAD_IN = 3 * RWKV_WIDTH + 2 * LANES
RWKV_GN_EPS = 64e-5
DECAY_SCALE = math.exp(-0.5)
VMEM_LIMIT = 56 * 1024 * 1024


def _bf(x):
    return x.astype(BF16)


def _dot(a, b):
    return jnp.dot(_bf(a), _bf(b), preferred_element_type=F32)


def _dot_nt(a, b):
    return lax.dot_general(_bf(a), _bf(b), (((1,), (1,)), ((), ())), preferred_element_type=F32)


def _dot_tn(a, b):
    return lax.dot_general(_bf(a), _bf(b), (((0,), (0,)), ((), ())), preferred_element_type=F32)


def _split3(x):
    hi = _bf(x)
    r1 = x - hi.astype(F32)
    mid = _bf(r1)
    lo = _bf(r1 - mid.astype(F32))
    return hi, mid, lo


def _dot_exact_lhs(a_bf, x, terms=3):
    parts = _split3(x)[:terms]
    acc = jnp.dot(a_bf, parts[0], preferred_element_type=F32)
    for p in parts[1:]:
        acc = acc + jnp.dot(a_bf, p, preferred_element_type=F32)
    return acc


def _dot_exact_rhs(x, b_bf, terms=2):
    parts = _split3(x)[:terms]
    acc = jnp.dot(parts[0], b_bf, preferred_element_type=F32)
    for p in parts[1:]:
        acc = acc + jnp.dot(p, b_bf, preferred_element_type=F32)
    return acc


def _sigmoid(x):
    return 1.0 / (1.0 + jnp.exp(-x))


def _rwkv_kernel(pf_ref, pb_ref, wup_ref, aup_ref, gup_ref, w0_ref, a0_ref, kk_ref, ka_ref, rk_ref,
                 gng_ref, gnb_ref, hsum_ref, o_ref, state_ref, *, seq, chunk):
    C, N, H, W = chunk, HEAD_DIM, RWKV_HEADS, RWKV_WIDTH
    nc = seq // C
    c = pl.program_id(1)

    @pl.when(c == 0)
    def _():
        state_ref[...] = jnp.zeros_like(state_ref)

    hsum = hsum_ref[...]
    row = lax.broadcasted_iota(jnp.int32, (C, C), 0)
    col = lax.broadcasted_iota(jnp.int32, (C, C), 1)

    chunk_of = (c, nc - 1 - c)
    groups = [(bb, d) for d in range(2) for bb in range(RWKV_BATCH)]
    G = len(groups)
    rows_of = {g: slice(i * C, (i + 1) * C) for i, g in enumerate(groups)}
    stack = lambda parts: jnp.concatenate(parts, axis=0)

    def prepare(bb, d):
        ps = (pf_ref if d == 0 else pb_ref)[bb]
        r = ps[:, 0:W]
        k = ps[:, W:2 * W]
        v = ps[:, 2 * W:3 * W]
        x1 = ps[:, 3 * W:3 * W + LANES]
        x2 = ps[:, 3 * W + LANES:3 * W + 2 * LANES]
        a_both = [_sigmoid(a0_ref[e:e + 1, :] + _dot(x2, aup_ref[e])) for e in range(2)]
        w_logit = w0_ref[d:d + 1, :] + _dot(jnp.tanh(x1), wup_ref[d])
        logw = -DECAY_SCALE * _sigmoid(w_logit)
        kkr = k * kk_ref[...]
        ss = _dot_exact_rhs(kkr * kkr, hsum)
        kk = kkr / jnp.maximum(jnp.sqrt(ss), 1e-12)
        kdir_both = [k * (1.0 + (a - 1.0) * ka_ref[...]) for a in a_both]
        kdir = kdir_both[d]
        b = kk * a_both[d]
        incl = (row >= col) if d == 0 else (row <= col)
        tri = jnp.where(incl, 1.0, 0.0).astype(BF16)
        lin = _dot_exact_lhs(tri, logw)
        tot = jnp.sum(logw, axis=0, keepdims=True)
        e_neg = jnp.exp(-lin)
        e_tot = jnp.exp(tot)
        kh = kdir * e_neg
        bh = b * e_neg
        return dict(r=r, v=v, x1=x1, ksum=kdir_both[0] + kdir_both[1], rt=r * jnp.exp(lin),
                    at=kk * jnp.exp(lin - logw), kh=kh, bh=bh, kc=kh * e_tot, bc=bh * e_tot, e_tot=e_tot)

    pre = {g: prepare(*g) for g in groups}
    chains = [(bb, d, h) for bb, d in groups for h in range(H)]
    assert 2 * N == LANES and C == N
    r2 = lax.broadcasted_iota(jnp.int32, (2 * C, LANES), 0)
    l2 = lax.broadcasted_iota(jnp.int32, (2 * C, LANES), 1)
    t_row = jnp.where(r2 >= C, r2 - C, r2)
    t_col = jnp.where(l2 >= N, l2 - N, l2)
    before = (t_row > t_col, t_row < t_col)
    diag_r = jnp.logical_and(r2 >= C, t_row == t_col)
    tile_mask = [jnp.logical_or(before[d], diag_r) for d in range(2)]
    left = lax.broadcasted_iota(jnp.int32, (C, LANES), 1) < N
    zeros_cn = jnp.zeros((C, N), F32)
    zeros_cl = jnp.zeros((C, LANES), F32)

    s0, tile0, ws = {}, {}, {}
    for bb, d, h in chains:
        sl = slice(h * N, (h + 1) * N)
        q = pre[bb, d]
        lhs2 = jnp.concatenate([q['at'][:, sl], q['rt'][:, sl]], axis=0)
        s0[bb, d, h] = state_ref[bb, d, h]
        rhs3 = jnp.concatenate([q['kh'][:, sl], q['bh'][:, sl], s0[bb, d, h]], axis=0)
        g = _dot_nt(lhs2, rhs3)
        tile0[bb, d, h] = jnp.where(tile_mask[d], g[:, 0:LANES], 0.0)
        ws[bb, d, h] = g[:, LANES:LANES + N]
    av = {}
    for bb, d, h in chains:
        sl = slice(h * N, (h + 1) * N)
        lhs = jnp.where(l2 < N, tile0[bb, d, h], 0.0)
        av[bb, d, h] = _dot(lhs, jnp.concatenate([pre[bb, d]['v'][:, sl], zeros_cn], axis=0))
    z = {}
    for ch in chains:
        u0 = jnp.concatenate([ws[ch][0:C] + av[ch][0:C], zeros_cn], axis=1)
        z[ch] = jnp.where(left, u0, -tile0[ch][0:C])
    for _ in range(int(math.log2(C))):
        for ch in chains:
            prod = _dot(jnp.where(left, 0.0, z[ch]), jnp.concatenate([zeros_cl, z[ch]], axis=0))
            z[ch] = jnp.where(left, z[ch] + prod, prod)
    ys = {}
    for bb, d, h in chains:
        ch = (bb, d, h)
        sl = slice(h * N, (h + 1) * N)
        q = pre[bb, d]
        u = z[ch][:, 0:N]
        rb = jnp.where(left, 0.0, tile0[ch][C:2 * C])
        ys[ch] = ws[ch][C:2 * C] + av[ch][C:2 * C] - _dot(rb, jnp.concatenate([zeros_cn, u], axis=0))
        upd = _dot_tn(jnp.concatenate([q['v'][:, sl], u], axis=0),
                      jnp.concatenate([q['kc'][:, sl], -q['bc'][:, sl]], axis=0))
        state_ref[bb, d, h] = s0[ch] * q['e_tot'][:, sl] + upd

    y = {g: jnp.concatenate([ys[g + (h,)] for h in range(H)], axis=1) for g in groups}
    base = [pl.multiple_of(chunk_of[d] * C, C) for d in range(2)]

    @pl.when(c < nc // 2)
    def _():
        for bb, d in groups:
            o_ref[bb, pl.ds(base[d], C), :] = y[bb, d]

    @pl.when(c >= nc // 2)
    def _():
        inv_n = 1.0 / N
        yt = stack([o_ref[bb, pl.ds(base[d], C), :] + y[bb, d] for bb, d in groups])
        bonus_in = stack([pre[g]['r'] * rk_ref[...] * pre[g]['ksum'] for g in groups])
        sums = _dot_exact_rhs(stack([yt, bonus_in]), hsum)
        yc = yt - sums[0:G * C] * inv_n
        coef = sums[G * C:2 * G * C]
        var = _dot_exact_rhs(yc * yc, hsum) * inv_n
        yn = yc * lax.rsqrt(var + RWKV_GN_EPS) * gng_ref[...] + gnb_ref[...]
        gate = _dot(_sigmoid(stack([pre[g]['x1'] for g in groups])), gup_ref[...])
        v_all = stack([pre[g]['v'] for g in groups])
        out = (yn + coef * v_all) * gate
        for bb, d in groups:
            o_ref[bb, pl.ds(base[d], C), :] = out[rows_of[bb, d]]


def _rwkv_mixer(p_shift, w0, w_up, a0, a_up, g_up, k_k, k_a, r_k, gn_g, gn_b):
    B, S, PW = p_shift.shape
    W = RWKV_WIDTH
    C = RWKV_CHUNK
    BB = RWKV_BATCH
    nc = S // C
    assert B % BB == 0 and nc % 2 == 0
    wup = jnp.zeros((2, LANES, W), F32)
    for d in range(2):
        lo = GATE_LORA + d * DECAY_LORA
        wup = wup.at[d, lo:lo + DECAY_LORA].set(w_up[d])
    aup = jnp.zeros((2, LANES, W), F32)
    for d in range(2):
        aup = aup.at[d, d * ICLR_LORA:(d + 1) * ICLR_LORA].set(a_up[d])
    gup = jnp.zeros((LANES, W), F32).at[:GATE_LORA].set(g_up)
    head = jnp.arange(W) // HEAD_DIM
    hsum = (head[:, None] == head[None, :]).astype(BF16)
    row = lambda t: t.reshape(1, W)
    const2 = lambda shape: pl.BlockSpec(shape, lambda b, c: (0,) * len(shape))
    return pl.pallas_call(
        functools.partial(_rwkv_kernel, seq=S, chunk=C),
        grid=(B // BB, nc),
        in_specs=[pl.BlockSpec((BB, C, PW), lambda b, c: (b, c, 0)),
                  pl.BlockSpec((BB, C, PW), lambda b, c: (b, nc - 1 - c, 0)),
                  const2((2, LANES, W)), const2((2, LANES, W)), const2((LANES, W)),
                  const2((2, W)), const2((2, W)), const2((1, W)), const2((1, W)), const2((1, W)),
                  const2((1, W)), const2((1, W)), const2((W, W))],
        out_specs=pl.BlockSpec((BB, S, W), lambda b, c: (b, 0, 0)),
        out_shape=jax.ShapeDtypeStruct((B, S, W), F32),
        scratch_shapes=[pltpu.VMEM((BB, 2, RWKV_HEADS, HEAD_DIM, HEAD_DIM), F32)],
        compiler_params=pltpu.CompilerParams(dimension_semantics=("arbitrary", "arbitrary"),
                                             vmem_limit_bytes=VMEM_LIMIT),
        name="rwkv7_scan",
    )(p_shift, p_shift, _bf(wup), _bf(aup), _bf(gup), w0, a0, row(k_k), row(k_a),
      r_k.reshape(1, W), row(gn_g), row(gn_b), hsum)


ATT_HEADS = 6
ATT_WIDTH = ATT_HEADS * HEAD_DIM
DILATED_BRANCHES = ((128, 1), (512, 4), (2048, 16))
ATT_BLOCK = 64
NEG_INF = -1e30
ATT_UNROLL = 16


def _rows(start, size, stride):
    return pl.ds(start, size) if stride == 1 else pl.ds(start, size, stride=stride)


def _att_kernel(slope_ref, q_ref, k_ref, v_ref, out_ref, o_s, lse_s, *, seq):
    S, Q = seq, ATT_BLOCK
    left = lax.broadcasted_iota(jnp.int32, (1, LANES), 1) < HEAD_DIM
    qi = lax.broadcasted_iota(jnp.int32, (Q, 3 * Q), 0)
    kj = lax.broadcasted_iota(jnp.int32, (Q, 3 * Q), 1)
    dist = jnp.abs(kj - Q - qi).astype(F32)
    band = dist <= Q
    kj_row = lax.broadcasted_iota(jnp.int32, (1, 3 * Q), 1)
    ones_v = jnp.ones((3 * Q, LANES), BF16)
    for bi, (window, dil) in enumerate(DILATED_BRANCHES):
        assert window // (2 * dil) == Q and S % (Q * dil) == 0
        nb = S // (Q * dil)
        bias = jnp.concatenate([jnp.where(band, (-dil * slope_ref[0, hh:hh + 1, 0:1]) * dist, NEG_INF)
                                for hh in range(2)], axis=0)

        def body(it, carry, dil=dil, nb=nb, bias=bias, bi=bi):
            blocks = []
            whole_classes = ATT_UNROLL % nb == 0
            if whole_classes:
                starts = [(it * ATT_UNROLL + uu) // nb + dil * Q * (uu % nb) for uu in range(ATT_UNROLL)]
                k_own = [_bf(k_ref[0, _rows(s, Q, dil), :]) for s in starts]
                v_own = [_bf(v_ref[0, _rows(s, Q, dil), :]) for s in starts]
            for uu in range(ATT_UNROLL):
                if whole_classes:
                    n = uu % nb
                    start = starts[uu]
                    lo = uu - 1 if n > 0 else uu
                    hi = uu + 1 if n < nb - 1 else uu
                    kw = jnp.concatenate([k_own[lo], k_own[uu], k_own[hi]], axis=0)
                    vw = jnp.concatenate([v_own[lo], v_own[uu], v_own[hi]], axis=0)
                else:
                    i = it * ATT_UNROLL + uu
                    n = i % nb
                    start = pl.multiple_of(i * Q, Q) if dil == 1 else i // nb + dil * Q * n
                    s_prev = start - jnp.where(n > 0, dil * Q, 0)
                    s_next = start + jnp.where(n < nb - 1, dil * Q, 0)
                    if dil == 1:
                        s_prev, s_next = pl.multiple_of(s_prev, Q), pl.multiple_of(s_next, Q)
                    kw = _bf(jnp.concatenate([k_ref[0, _rows(s, Q, dil), :] for s in (s_prev, start, s_next)], axis=0))
                    vw = _bf(jnp.concatenate([v_ref[0, _rows(s, Q, dil), :] for s in (s_prev, start, s_next)], axis=0))
                q = q_ref[0, _rows(start, Q, dil), :] * (HEAD_DIM ** -0.5)
                pen_prev = jnp.where(n == 0, NEG_INF, 0.0)
                pen_next = jnp.where(n == nb - 1, NEG_INF, 0.0)
                edge = jnp.where(kj_row < Q, pen_prev, jnp.where(kj_row >= 2 * Q, pen_next, 0.0))
                blocks.append((start, q, kw, jnp.concatenate([vw, ones_v], axis=1), edge))
            scores = [_dot_nt(jnp.concatenate([jnp.where(left, q, 0.0), jnp.where(left, 0.0, q)], axis=0), kw)
                      + bias + edge for (_, q, kw, _, edge) in blocks]
            mx = [jnp.max(s, axis=-1, keepdims=True) for s in scores]
            pr = [jnp.exp(s - m) for s, m in zip(scores, mx)]
            pv = [jnp.dot(_bf(p), blk[3], preferred_element_type=F32) for p, blk in zip(pr, blocks)]
            for uu, blk in enumerate(blocks):
                rows = _rows(blk[0], Q, dil)
                den = jnp.where(left, pv[uu][0:Q, LANES:2 * LANES], pv[uu][Q:2 * Q, LANES:2 * LANES])
                o_s[bi, rows, :] = jnp.where(left, pv[uu][0:Q, 0:LANES], pv[uu][Q:2 * Q, 0:LANES]) / den
                lse_s[bi, rows, :] = jnp.where(left, mx[uu][0:Q], mx[uu][Q:2 * Q]) + jnp.log(den)
            return carry

        lax.fori_loop(0, S // Q // ATT_UNROLL, body, 0)

    def merge(j, carry):
        rows = pl.ds(pl.multiple_of(j * Q, Q), Q)
        ms = [lse_s[b, rows, :] for b in range(3)]
        top = jnp.maximum(jnp.maximum(ms[0], ms[1]), ms[2])
        ws = [jnp.exp(m - top) for m in ms]
        num = ws[0] * o_s[0, rows, :] + ws[1] * o_s[1, rows, :] + ws[2] * o_s[2, rows, :]
        out_ref[0, rows, :] = num / (ws[0] + ws[1] + ws[2])
        return carry

    lax.fori_loop(0, S // Q, merge, 0)


def _dilated_attention(q, k, v):
    B, S, W = q.shape
    pairs = W // LANES
    slopes = jnp.exp2(-8.0 * jnp.arange(1, ATT_HEADS + 1, dtype=F32) / ATT_HEADS).reshape(pairs, 2, 1)
    slopes = jnp.broadcast_to(jnp.pad(slopes, ((0, 0), (0, 6), (0, 0))), (pairs, 8, LANES))
    spec = pl.BlockSpec((1, S, LANES), lambda b, p: (b, 0, p))
    return pl.pallas_call(
        functools.partial(_att_kernel, seq=S),
        grid=(B, pairs),
        in_specs=[pl.BlockSpec((1, 8, LANES), lambda b, p: (p, 0, 0)), spec, spec, spec],
        out_specs=spec,
        out_shape=jax.ShapeDtypeStruct((B, S, W), F32),
        scratch_shapes=[pltpu.VMEM((3, S, LANES), F32)] * 2,
        compiler_params=pltpu.CompilerParams(dimension_semantics=("arbitrary", "arbitrary"),
                                             vmem_limit_bytes=VMEM_LIMIT),
        name="dilated_attention",
    )(slopes, q, k, v)


N_EXPERTS = 32
TOP_K = 4
SWIGLU_LIMIT = 7.0
SWIGLU_ALPHA = 1.702
MOE_ROWS = 512


CAST_ROWS = 128


def _expert_kernel(be_ref, nb_ref, x_ref, wgu_ref, bgu_ref, wd_ref, bd_ref, o_ref, wgu_s, wd_s):
    i = pl.program_id(0)
    F = wd_ref.shape[1]
    used = i < nb_ref[0]

    @pl.when(used & ((i == 0) | (be_ref[i] != be_ref[jnp.maximum(i - 1, 0)])))
    def _():
        def cast(src, dst):
            def step(j, carry):
                rows = pl.ds(pl.multiple_of(j * CAST_ROWS, CAST_ROWS), CAST_ROWS)
                dst[rows, :] = _bf(src[0, rows, :])
                return carry
            lax.fori_loop(0, src.shape[1] // CAST_ROWS, step, 0)
        cast(wgu_ref, wgu_s)
        cast(wd_ref, wd_s)

    @pl.when(used)
    def _():
        gu = jnp.dot(x_ref[...], wgu_s[...], preferred_element_type=F32) + bgu_ref[0]
        gate = jnp.minimum(gu[:, :F], SWIGLU_LIMIT)
        up = jnp.clip(gu[:, F:], -SWIGLU_LIMIT, SWIGLU_LIMIT)
        act = (up + 1.0) * gate * _sigmoid(SWIGLU_ALPHA * gate)
        o_ref[...] = (jnp.dot(_bf(act), wd_s[...], preferred_element_type=F32) + bd_ref[0]).astype(o_ref.dtype)

    @pl.when(jnp.logical_not(used))
    def _():
        o_ref[...] = jnp.zeros_like(o_ref)


def _expert_blocks(block_e, n_used, x_rows, w_gate_up, b_gate_up, w_down, b_down):
    n_rows, D = x_rows.shape
    E, _, F2 = w_gate_up.shape
    F = F2 // 2
    n_blocks = n_rows // MOE_ROWS
    return pl.pallas_call(
        _expert_kernel,
        grid_spec=pltpu.PrefetchScalarGridSpec(
            num_scalar_prefetch=2,
            grid=(n_blocks,),
            in_specs=[pl.BlockSpec((MOE_ROWS, D), lambda i, be, nb: (i, 0)),
                      pl.BlockSpec((1, D, F2), lambda i, be, nb: (be[i], 0, 0)),
                      pl.BlockSpec((1, 1, F2), lambda i, be, nb: (be[i], 0, 0)),
                      pl.BlockSpec((1, F, D), lambda i, be, nb: (be[i], 0, 0)),
                      pl.BlockSpec((1, 1, D), lambda i, be, nb: (be[i], 0, 0))],
            out_specs=pl.BlockSpec((MOE_ROWS, D), lambda i, be, nb: (i, 0)),
            scratch_shapes=[pltpu.VMEM((D, F2), BF16), pltpu.VMEM((F, D), BF16)]),
        out_shape=jax.ShapeDtypeStruct((n_rows, D), BF16),
        compiler_params=pltpu.CompilerParams(dimension_semantics=("arbitrary",),
                                             vmem_limit_bytes=VMEM_LIMIT),
        name="moe_experts",
    )(block_e, n_used, x_rows, w_gate_up, b_gate_up.reshape(E, 1, F2), w_down, b_down.reshape(E, 1, D))


def _route(top_e, n_tokens):
    T = n_tokens
    n_rows = T * TOP_K + N_EXPERTS * MOE_ROWS
    n_blocks = n_rows // MOE_ROWS
    experts = jnp.arange(N_EXPERTS, dtype=jnp.int32)
    onehot = jnp.sum((top_e[:, :, None] == experts).astype(jnp.int32), axis=1)
    incl = jnp.cumsum(onehot, axis=0)
    counts = incl[-1]
    padded = (counts + MOE_ROWS - 1) // MOE_ROWS * MOE_ROWS
    starts = jnp.cumsum(counts) - counts
    pends = jnp.cumsum(padded)
    pstarts = pends - padded
    rank = jnp.take_along_axis(incl - onehot, top_e, axis=1)
    dest = pstarts[top_e] + rank
    order = jnp.argsort(top_e.reshape(-1), stable=True).astype(jnp.int32)
    block_start = jnp.arange(n_blocks, dtype=jnp.int32) * MOE_ROWS
    block_e = jnp.minimum(jnp.sum((pends[None, :] <= block_start[:, None]).astype(jnp.int32), axis=1),
                          N_EXPERTS - 1)
    row = jnp.arange(n_rows, dtype=jnp.int32)
    within = row - jnp.repeat(pstarts[block_e], MOE_ROWS)
    src = jnp.clip(within + jnp.repeat(starts[block_e], MOE_ROWS), 0, T * TOP_K - 1)
    row_token = jnp.where(within < jnp.repeat(counts[block_e], MOE_ROWS),
                          jnp.take(order, src, mode='clip') // TOP_K, row % T)
    n_used = (pends[-1:] // MOE_ROWS).astype(jnp.int32)
    return dest.astype(jnp.int32), row_token.astype(jnp.int32), block_e.astype(jnp.int32), n_used


LN_EPS = 1e-5
MEM_HEADS = 4
MEM_WIDTH = MEM_HEADS * HEAD_DIM
RWKV_IN = 3 * RWKV_WIDTH + GATE_LORA + 2 * DECAY_LORA + 2 * ICLR_LORA
PROJ_ROWS = 512
BATCH_GROUPS = 1


def _ln(x, g, b):
    mu = jnp.mean(x, -1, keepdims=True)
    xc = x - mu
    var = jnp.mean(xc * xc, -1, keepdims=True)
    return xc * lax.rsqrt(var + LN_EPS) * g + b


HALO = 8


def _in_proj_kernel(x_ref, xp_ref, xn_ref, g_ref, b_ref, w_ref, mu_ref, rw_ref, q_ref, k_ref, v_ref, qm_ref, *, seq):
    R = PROJ_ROWS
    i = pl.program_id(0)
    xs = jnp.concatenate([x_ref[...], xp_ref[...], xn_ref[...]], axis=0)
    hb = _bf(_ln(xs, g_ref[...], b_ref[...]))
    n_rw = rw_ref.shape[-1]
    p = jnp.dot(hb, w_ref[:, 0:n_rw], preferred_element_type=F32)
    cur = p[0:R]
    tiles_per_seq = seq // R
    first = (i % tiles_per_seq) == 0
    last = (i % tiles_per_seq) == tiles_per_seq - 1
    prev_row = jnp.where(first, 0.0, p[R + HALO - 1:R + HALO])
    next_row = jnp.where(last, 0.0, p[R + HALO:R + HALO + 1])
    rowi = lax.broadcasted_iota(jnp.int32, (R, 1), 0)
    prev = jnp.where(rowi == 0, prev_row, pltpu.roll(cur, 1, 0))
    nxt = jnp.where(rowi == R - 1, next_row, pltpu.roll(cur, R - 1, 0))
    rw_ref[...] = cur + mu_ref[...] * (0.5 * (prev + nxt) - cur)
    o = n_rw
    hb_main = hb[0:R]
    for ref in (q_ref, k_ref, v_ref, qm_ref):
        n = ref.shape[-1]
        ref[...] = jnp.dot(hb_main, w_ref[:, o:o + n], preferred_element_type=F32)
        o += n


def _in_proj(xt, ln_g, ln_b, w_pad, mu_pad, seq):
    T, D = xt.shape
    widths = (RWKV_PAD_IN, ATT_WIDTH, ATT_WIDTH, ATT_WIDTH, MEM_WIDTH)
    assert sum(widths) == w_pad.shape[1] and seq % PROJ_ROWS == 0
    per_tile = PROJ_ROWS // HALO
    last_halo = T // HALO - 1
    rows = lambda n: pl.BlockSpec((PROJ_ROWS, n), lambda i: (i, 0))
    full = lambda a: pl.BlockSpec(a.shape, lambda i: (0,) * a.ndim)
    consts = (ln_g.reshape(1, D), ln_b.reshape(1, D), w_pad, mu_pad.reshape(1, RWKV_PAD_IN))
    return pl.pallas_call(
        functools.partial(_in_proj_kernel, seq=seq),
        grid=(T // PROJ_ROWS,),
        in_specs=[rows(D),
                  pl.BlockSpec((HALO, D), lambda i: (jnp.maximum(i * per_tile - 1, 0), 0)),
                  pl.BlockSpec((HALO, D), lambda i: (jnp.minimum((i + 1) * per_tile, last_halo), 0))]
                 + [full(a) for a in consts],
        out_specs=[rows(n) for n in widths],
        out_shape=[jax.ShapeDtypeStruct((T, n), F32) for n in widths],
        compiler_params=pltpu.CompilerParams(dimension_semantics=("arbitrary",), vmem_limit_bytes=VMEM_LIMIT),
        name="in_proj",
    )(xt, xt, xt, *consts)


def _mem_att_kernel(q_ref, mem_ref, wkv_ref, o_ref, kv_s):
    W = MEM_WIDTH

    @pl.when(pl.program_id(1) == 0)
    def _():
        kv_s[...] = _bf(jnp.dot(_bf(mem_ref[0]), wkv_ref[...], preferred_element_type=F32))

    q = q_ref[0] * (HEAD_DIM ** -0.5)
    lane = lax.broadcasted_iota(jnp.int32, (1, W), 1)
    kmat = kv_s[:, 0:W]
    vmat = kv_s[:, W:2 * W]
    out = jnp.zeros(q.shape, F32)
    for h in range(MEM_HEADS):
        sel = (lane >= h * HEAD_DIM) & (lane < (h + 1) * HEAD_DIM)
        s = lax.dot_general(_bf(jnp.where(sel, q, 0.0)), kmat, (((1,), (1,)), ((), ())),
                            preferred_element_type=F32)
        p = jnp.exp(s - jnp.max(s, axis=-1, keepdims=True))
        pv = jnp.dot(_bf(p), vmat, preferred_element_type=F32)
        out = jnp.where(sel, pv / jnp.sum(p, axis=-1, keepdims=True), out)
    o_ref[0] = out


def _mem_attention(q_m, mem, w_kv):
    B, S, W = q_m.shape
    _, M, D = mem.shape
    return pl.pallas_call(
        _mem_att_kernel,
        grid=(B, S // PROJ_ROWS),
        in_specs=[pl.BlockSpec((1, PROJ_ROWS, W), lambda b, i: (b, i, 0)),
                  pl.BlockSpec((1, M, D), lambda b, i: (b, 0, 0)),
                  pl.BlockSpec((D, 2 * W), lambda b, i: (0, 0))],
        out_specs=pl.BlockSpec((1, PROJ_ROWS, W), lambda b, i: (b, i, 0)),
        out_shape=jax.ShapeDtypeStruct((B, S, W), F32),
        scratch_shapes=[pltpu.VMEM((M, 2 * W), BF16)],
        compiler_params=pltpu.CompilerParams(dimension_semantics=("arbitrary", "arbitrary"),
                                             vmem_limit_bytes=VMEM_LIMIT),
        name="memory_attention",
    )(q_m, mem, _bf(w_kv))


def _out_proj_kernel(x_ref, yr_ref, ya_ref, ym_ref, g0_ref, b0_ref, w_ref, g1_ref, b1_ref, wrh_ref, wrl_ref,
                     br_ref, h1_ref, h1b_ref, gate_ref, expert_ref, *, alpha):
    h0 = _ln(x_ref[...], g0_ref[...], b0_ref[...])
    o = 0
    mix = None
    for ref in (yr_ref, ya_ref, ym_ref):
        n = ref.shape[-1]
        part = jnp.dot(_bf(ref[...]), w_ref[o:o + n, :], preferred_element_type=F32)
        mix = part if mix is None else mix + part
        o += n
    h1 = _ln(alpha * h0 + mix, g1_ref[...], b1_ref[...])
    h1_ref[...] = h1
    hi = _bf(h1)
    h1b_ref[...] = hi
    lo = _bf(h1 - hi.astype(F32))
    logits = (jnp.dot(hi, wrh_ref[...], preferred_element_type=F32)
              + jnp.dot(lo, wrh_ref[...], preferred_element_type=F32)
              + jnp.dot(hi, wrl_ref[...], preferred_element_type=F32)) + br_ref[...]
    lane = lax.broadcasted_iota(jnp.int32, logits.shape, 1).astype(F32)
    vals = logits
    top_v, top_i = [], []
    for _ in range(TOP_K):
        m = jnp.max(vals, axis=-1, keepdims=True)
        idx = jnp.min(jnp.where(vals == m, lane, float(LANES)), axis=-1, keepdims=True)
        top_v.append(m)
        top_i.append(idx)
        vals = jnp.where(lane == idx, -jnp.inf, vals)
    ex = [jnp.exp(v - top_v[0]) for v in top_v]
    den = ex[0] + ex[1] + ex[2] + ex[3]
    gates = jnp.zeros(logits.shape, F32)
    experts = jnp.zeros(logits.shape, F32)
    for k in range(TOP_K):
        gates = jnp.where(lane == k, ex[k] / den, gates)
        experts = jnp.where(lane == k, top_i[k], experts)
    gate_ref[...] = gates
    expert_ref[...] = experts.astype(jnp.int32)


def _out_proj(xt, y_rwkv, y_att, y_mem, ln0_g, ln0_b, w_out, ln1_g, ln1_b, w_router, b_router, alpha):
    T, D = xt.shape
    E = w_router.shape[1]
    wr = jnp.pad(w_router, ((0, 0), (0, LANES - E)))
    wr_hi = _bf(wr)
    wr_lo = _bf(wr - wr_hi.astype(F32))
    br = jnp.pad(b_router, (0, LANES - E), constant_values=NEG_INF).reshape(1, LANES)
    rows = lambda n: pl.BlockSpec((PROJ_ROWS, n), lambda i: (i, 0))
    full = lambda a: pl.BlockSpec(a.shape, lambda i: (0,) * a.ndim)
    vec = lambda t: t.reshape(1, D)
    consts = (vec(ln0_g), vec(ln0_b), _bf(w_out), vec(ln1_g), vec(ln1_b), wr_hi, wr_lo, br)
    return pl.pallas_call(
        functools.partial(_out_proj_kernel, alpha=alpha),
        grid=(T // PROJ_ROWS,),
        in_specs=[rows(D), rows(y_rwkv.shape[1]), rows(y_att.shape[1]), rows(y_mem.shape[1])]
                 + [full(a) for a in consts],
        out_specs=[rows(D), rows(D), rows(LANES), rows(LANES)],
        out_shape=[jax.ShapeDtypeStruct((T, D), F32), jax.ShapeDtypeStruct((T, D), BF16),
                   jax.ShapeDtypeStruct((T, LANES), F32), jax.ShapeDtypeStruct((T, LANES), jnp.int32)],
        compiler_params=pltpu.CompilerParams(dimension_semantics=("arbitrary",), vmem_limit_bytes=VMEM_LIMIT),
        name="out_proj_router",
    )(xt, y_rwkv, y_att, y_mem, *consts)


def _combine_kernel(h1_ref, y_ref, gate_ref, g_ref, b_ref, o_ref, *, alpha):
    acc = alpha * h1_ref[...]
    gates = gate_ref[...]
    for k in range(TOP_K):
        acc = acc + gates[:, k:k + 1] * y_ref[k].astype(F32)
    o_ref[...] = _ln(acc, g_ref[...], b_ref[...])


def _combine(h1, y_top, gates, ln_g, ln_b, alpha):
    T, D = h1.shape
    rows = lambda n: pl.BlockSpec((PROJ_ROWS, n), lambda i: (i, 0))
    full = lambda a: pl.BlockSpec(a.shape, lambda i: (0,) * a.ndim)
    consts = (ln_g.reshape(1, D), ln_b.reshape(1, D))
    return pl.pallas_call(
        functools.partial(_combine_kernel, alpha=alpha),
        grid=(T // PROJ_ROWS,),
        in_specs=[rows(D), pl.BlockSpec((TOP_K, PROJ_ROWS, D), lambda i: (0, i, 0)), rows(LANES)]
                 + [full(a) for a in consts],
        out_specs=rows(D),
        out_shape=jax.ShapeDtypeStruct((T, D), F32),
        compiler_params=pltpu.CompilerParams(dimension_semantics=("arbitrary",), vmem_limit_bytes=VMEM_LIMIT),
        name="combine_ln",
    )(h1, y_top, gates, *consts)


def kernel(x, mem, ln_in_g, ln_in_b, w_in, mu_shift, w0, w_up, a0, a_up, g_up, k_k, k_a, r_k, gn_g, gn_b, w_mem_kv, w_out, ln1_g, ln1_b, w_router, b_router, w_gate_up, b_gate_up, w_down, b_down, ln2_g, ln2_b):
    B, S, D = x.shape
    depth = w_in.shape[0]
    assert depth == 1, "the layer norm feeding a layer is fused into its projection kernels"
    assert B % BATCH_GROUPS == 0
    alpha = (2 * depth) ** 0.25
    l = 0
    pad = RWKV_PAD_IN - RWKV_IN
    w_pad = _bf(jnp.concatenate([w_in[l][:, :RWKV_IN], jnp.zeros((D, pad), F32), w_in[l][:, RWKV_IN:]], axis=1))
    mu_pad = jnp.pad(mu_shift[l], (0, pad))

    def layer(xg, memg):
        Bg = xg.shape[0]
        T = Bg * S
        xt = xg.reshape(T, D)
        p_rwkv, q_a, k_a_, v_a, q_m = _in_proj(xt, ln_in_g, ln_in_b, w_pad, mu_pad, S)
        seq = lambda t: t.reshape(Bg, S, t.shape[-1])
        y_rwkv = _rwkv_mixer(seq(p_rwkv), w0[l], w_up[l], a0[l], a_up[l], g_up[l], k_k[l], k_a[l], r_k[l],
                             gn_g[l], gn_b[l])
        y_att = _dilated_attention(seq(q_a), seq(k_a_), seq(v_a))
        y_mem = _mem_attention(seq(q_m), memg, w_mem_kv[l])
        flat = lambda t: t.reshape(T, t.shape[-1])
        h1, h1b, gates, experts = _out_proj(xt, flat(y_rwkv), flat(y_att), flat(y_mem), ln_in_g, ln_in_b, w_out[l],
                                            ln1_g[l], ln1_b[l], w_router[l], b_router[l], alpha)
        dest, row_token, block_e, n_used = _route(experts[:, :TOP_K], T)
        x_rows = jnp.take(h1b, row_token, axis=0, mode='clip')
        y_rows = _expert_blocks(block_e, n_used, x_rows, w_gate_up[l], b_gate_up[l], w_down[l], b_down[l])
        y_top = jnp.take(y_rows, dest.T.reshape(-1), axis=0, mode='clip').reshape(TOP_K, T, D)
        return _combine(h1, y_top, gates, ln2_g[l], ln2_b[l], alpha).reshape(Bg, S, D)

    Bg = B // BATCH_GROUPS
    outs = [layer(x[g * Bg:(g + 1) * Bg], mem[g * Bg:(g + 1) * Bg]) for g in range(BATCH_GROUPS)]
    return outs[0] if BATCH_GROUPS == 1 else jnp.concatenate(outs, axis=0)
```

```python
import functools
import math

import jax
import jax.numpy as jnp
from jax import lax
from jax.experimental import pallas as pl
from jax.experimental.pallas import tpu as pltpu

F32 = jnp.float32
BF16 = jnp.bfloat16

HEAD_DIM = 64
RWKV_HEADS = 6
RWKV_WIDTH = RWKV_HEADS * HEAD_DIM
GATE_LORA = 64
DECAY_LORA = 32
ICLR_LORA = 32
LANES = 128
RWKV_CHUNK = 64
RWKV_BATCH = 4
RWKV_PAD_IN = 3 * RWKV_WIDTH + 2 * LANES
RWKV_GN_EPS = 64e-5
DECAY_SCALE = math.exp(-0.5)
VMEM_LIMIT = 56 * 1024 * 1024


def _bf(x):
    return x.astype(BF16)


def _dot(a, b):
    return jnp.dot(_bf(a), _bf(b), preferred_element_type=F32)


def _dot_nt(a, b):
    return lax.dot_general(_bf(a), _bf(b), (((1,), (1,)), ((), ())), preferred_element_type=F32)


def _dot_tn(a, b):
    return lax.dot_general(_bf(a), _bf(b), (((0,), (0,)), ((), ())), preferred_element_type=F32)


def _split3(x):
    hi = _bf(x)
    r1 = x - hi.astype(F32)
    mid = _bf(r1)
    lo = _bf(r1 - mid.astype(F32))
    return hi, mid, lo


def _dot_exact_lhs(a_bf, x, terms=3):
    parts = _split3(x)[:terms]
    acc = jnp.dot(a_bf, parts[0], preferred_element_type=F32)
    for p in parts[1:]:
        acc = acc + jnp.dot(a_bf, p, preferred_element_type=F32)
    return acc


def _dot_exact_rhs(x, b_bf, terms=2):
    parts = _split3(x)[:terms]
    acc = jnp.dot(parts[0], b_bf, preferred_element_type=F32)
    for p in parts[1:]:
        acc = acc + jnp.dot(p, b_bf, preferred_element_type=F32)
    return acc


def _sigmoid(x):
    return 1.0 / (1.0 + jnp.exp(-x))


def _rwkv_kernel(pf_ref, pb_ref, wup_ref, aup_ref, gup_ref, w0_ref, a0_ref, kk_ref, ka_ref, rk_ref,
                 gng_ref, gnb_ref, hsum_ref, o_ref, state_ref, *, seq, chunk):
    C, N, H, W = chunk, HEAD_DIM, RWKV_HEADS, RWKV_WIDTH
    nc = seq // C
    c = pl.program_id(1)

    @pl.when(c == 0)
    def _():
        state_ref[...] = jnp.zeros_like(state_ref)

    hsum = hsum_ref[...]
    row = lax.broadcasted_iota(jnp.int32, (C, C), 0)
    col = lax.broadcasted_iota(jnp.int32, (C, C), 1)

    chunk_of = (c, nc - 1 - c)
    groups = [(bb, d) for d in range(2) for bb in range(RWKV_BATCH)]
    G = len(groups)
    rows_of = {g: slice(i * C, (i + 1) * C) for i, g in enumerate(groups)}
    stack = lambda parts: jnp.concatenate(parts, axis=0)

    def prepare(bb, d):
        ps = (pf_ref if d == 0 else pb_ref)[bb]
        r = ps[:, 0:W]
        k = ps[:, W:2 * W]
        v = ps[:, 2 * W:3 * W]
        x1 = ps[:, 3 * W:3 * W + LANES]
        x2 = ps[:, 3 * W + LANES:3 * W + 2 * LANES]
        a_both = [_sigmoid(a0_ref[e:e + 1, :] + _dot(x2, aup_ref[e])) for e in range(2)]
        w_logit = w0_ref[d:d + 1, :] + _dot(jnp.tanh(x1), wup_ref[d])
        logw = -DECAY_SCALE * _sigmoid(w_logit)
        kkr = k * kk_ref[...]
        ss = _dot_exact_rhs(kkr * kkr, hsum)
        kk = kkr / jnp.maximum(jnp.sqrt(ss), 1e-12)
        kdir_both = [k * (1.0 + (a - 1.0) * ka_ref[...]) for a in a_both]
        kdir = kdir_both[d]
        b = kk * a_both[d]
        incl = (row >= col) if d == 0 else (row <= col)
        tri = jnp.where(incl, 1.0, 0.0).astype(BF16)
        lin = _dot_exact_lhs(tri, logw)
        tot = jnp.sum(logw, axis=0, keepdims=True)
        e_neg = jnp.exp(-lin)
        e_tot = jnp.exp(tot)
        kh = kdir * e_neg
        bh = b * e_neg
        return dict(r=r, v=v, x1=x1, ksum=kdir_both[0] + kdir_both[1], rt=r * jnp.exp(lin),
                    at=kk * jnp.exp(lin - logw), kh=kh, bh=bh, kc=kh * e_tot, bc=bh * e_tot, e_tot=e_tot)

    pre = {g: prepare(*g) for g in groups}
    chains = [(bb, d, h) for bb, d in groups for h in range(H)]
    assert 2 * N == LANES and C == N
    r2 = lax.broadcasted_iota(jnp.int32, (2 * C, LANES), 0)
    l2 = lax.broadcasted_iota(jnp.int32, (2 * C, LANES), 1)
    t_row = jnp.where(r2 >= C, r2 - C, r2)
    t_col = jnp.where(l2 >= N, l2 - N, l2)
    before = (t_row > t_col, t_row < t_col)
    diag_r = jnp.logical_and(r2 >= C, t_row == t_col)
    tile_mask = [jnp.logical_or(before[d], diag_r) for d in range(2)]
    left = lax.broadcasted_iota(jnp.int32, (C, LANES), 1) < N
    zeros_cn = jnp.zeros((C, N), F32)
    zeros_cl = jnp.zeros((C, LANES), F32)

    s0, tile0, ws = {}, {}, {}
    for bb, d, h in chains:
        sl = slice(h * N, (h + 1) * N)
        q = pre[bb, d]
        lhs2 = jnp.concatenate([q['at'][:, sl], q['rt'][:, sl]], axis=0)
        s0[bb, d, h] = state_ref[bb, d, h]
        rhs3 = jnp.concatenate([q['kh'][:, sl], q['bh'][:, sl], s0[bb, d, h]], axis=0)
        g = _dot_nt(lhs2, rhs3)
        tile0[bb, d, h] = jnp.where(tile_mask[d], g[:, 0:LANES], 0.0)
        ws[bb, d, h] = g[:, LANES:LANES + N]
    av = {}
    for bb, d, h in chains:
        sl = slice(h * N, (h + 1) * N)
        lhs = jnp.where(l2 < N, tile0[bb, d, h], 0.0)
        av[bb, d, h] = _dot(lhs, jnp.concatenate([pre[bb, d]['v'][:, sl], zeros_cn], axis=0))
    z = {}
    for ch in chains:
        u0 = jnp.concatenate([ws[ch][0:C] + av[ch][0:C], zeros_cn], axis=1)
        z[ch] = jnp.where(left, u0, -tile0[ch][0:C])
    for _ in range(int(math.log2(C))):
        for ch in chains:
            prod = _dot(jnp.where(left, 0.0, z[ch]), jnp.concatenate([zeros_cl, z[ch]], axis=0))
            z[ch] = jnp.where(left, z[ch] + prod, prod)
    ys = {}
    for bb, d, h in chains:
        ch = (bb, d, h)
        sl = slice(h * N, (h + 1) * N)
        q = pre[bb, d]
        u = z[ch][:, 0:N]
        rb = jnp.where(left, 0.0, tile0[ch][C:2 * C])
        ys[ch] = ws[ch][C:2 * C] + av[ch][C:2 * C] - _dot(rb, jnp.concatenate([zeros_cn, u], axis=0))
        upd = _dot_tn(jnp.concatenate([q['v'][:, sl], u], axis=0),
                      jnp.concatenate([q['kc'][:, sl], -q['bc'][:, sl]], axis=0))
        state_ref[bb, d, h] = s0[ch] * q['e_tot'][:, sl] + upd

    y = {g: jnp.concatenate([ys[g + (h,)] for h in range(H)], axis=1) for g in groups}
    base = [pl.multiple_of(chunk_of[d] * C, C) for d in range(2)]

    @pl.when(c < nc // 2)
    def _():
        for bb, d in groups:
            o_ref[bb, pl.ds(base[d], C), :] = y[bb, d]

    @pl.when(c >= nc // 2)
    def _():
        inv_n = 1.0 / N
        yt = stack([o_ref[bb, pl.ds(base[d], C), :] + y[bb, d] for bb, d in groups])
        bonus_in = stack([pre[g]['r'] * rk_ref[...] * pre[g]['ksum'] for g in groups])
        sums = _dot_exact_rhs(stack([yt, bonus_in]), hsum)
        yc = yt - sums[0:G * C] * inv_n
        coef = sums[G * C:2 * G * C]
        var = _dot_exact_rhs(yc * yc, hsum) * inv_n
        yn = yc * lax.rsqrt(var + RWKV_GN_EPS) * gng_ref[...] + gnb_ref[...]
        gate = _dot(_sigmoid(stack([pre[g]['x1'] for g in groups])), gup_ref[...])
        v_all = stack([pre[g]['v'] for g in groups])
        out = (yn + coef * v_all) * gate
        for bb, d in groups:
            o_ref[bb, pl.ds(base[d], C), :] = out[rows_of[bb, d]]


def _rwkv_mixer(p_shift, w0, w_up, a0, a_up, g_up, k_k, k_a, r_k, gn_g, gn_b):
    B, S, PW = p_shift.shape
    W = RWKV_WIDTH
    C = RWKV_CHUNK
    BB = RWKV_BATCH
    nc = S // C
    assert B % BB == 0 and nc % 2 == 0
    wup = jnp.zeros((2, LANES, W), F32)
    for d in range(2):
        lo = GATE_LORA + d * DECAY_LORA
        wup = wup.at[d, lo:lo + DECAY_LORA].set(w_up[d])
    aup = jnp.zeros((2, LANES, W), F32)
    for d in range(2):
        aup = aup.at[d, d * ICLR_LORA:(d + 1) * ICLR_LORA].set(a_up[d])
    gup = jnp.zeros((LANES, W), F32).at[:GATE_LORA].set(g_up)
    head = jnp.arange(W) // HEAD_DIM
    hsum = (head[:, None] == head[None, :]).astype(BF16)
    row = lambda t: t.reshape(1, W)
    const2 = lambda shape: pl.BlockSpec(shape, lambda b, c: (0,) * len(shape))
    return pl.pallas_call(
        functools.partial(_rwkv_kernel, seq=S, chunk=C),
        grid=(B // BB, nc),
        in_specs=[pl.BlockSpec((BB, C, PW), lambda b, c: (b, c, 0)),
                  pl.BlockSpec((BB, C, PW), lambda b, c: (b, nc - 1 - c, 0)),
                  const2((2, LANES, W)), const2((2, LANES, W)), const2((LANES, W)),
                  const2((2, W)), const2((2, W)), const2((1, W)), const2((1, W)), const2((1, W)),
                  const2((1, W)), const2((1, W)), const2((W, W))],
        out_specs=pl.BlockSpec((BB, S, W), lambda b, c: (b, 0, 0)),
        out_shape=jax.ShapeDtypeStruct((B, S, W), F32),
        scratch_shapes=[pltpu.VMEM((BB, 2, RWKV_HEADS, HEAD_DIM, HEAD_DIM), F32)],
        compiler_params=pltpu.CompilerParams(dimension_semantics=("arbitrary", "arbitrary"),
                                             vmem_limit_bytes=VMEM_LIMIT),
        name="rwkv7_scan",
    )(p_shift, p_shift, _bf(wup), _bf(aup), _bf(gup), w0, a0, row(k_k), row(k_a),
      r_k.reshape(1, W), row(gn_g), row(gn_b), hsum)


ATT_HEADS = 6
ATT_WIDTH = ATT_HEADS * HEAD_DIM
DILATED_BRANCHES = ((128, 1), (512, 4), (2048, 16))
ATT_BLOCK = 64
NEG_INF = -1e30
ATT_UNROLL = 16


def _rows(start, size, stride):
    return pl.ds(start, size) if stride == 1 else pl.ds(start, size, stride=stride)


def _att_kernel(slope_ref, q_ref, k_ref, v_ref, out_ref, o_s, lse_s, *, seq):
    S, Q = seq, ATT_BLOCK
    left = lax.broadcasted_iota(jnp.int32, (1, LANES), 1) < HEAD_DIM
    qi = lax.broadcasted_iota(jnp.int32, (Q, 3 * Q), 0)
    kj = lax.broadcasted_iota(jnp.int32, (Q, 3 * Q), 1)
    dist = jnp.abs(kj - Q - qi).astype(F32)
    band = dist <= Q
    kj_row = lax.broadcasted_iota(jnp.int32, (1, 3 * Q), 1)
    ones_v = jnp.ones((3 * Q, LANES), BF16)
    for bi, (window, dil) in enumerate(DILATED_BRANCHES):
        assert window // (2 * dil) == Q and S % (Q * dil) == 0
        nb = S // (Q * dil)
        bias = jnp.concatenate([jnp.where(band, (-dil * slope_ref[0, hh:hh + 1, 0:1]) * dist, NEG_INF)
                                for hh in range(2)], axis=0)

        def body(it, carry, dil=dil, nb=nb, bias=bias, bi=bi):
            blocks = []
            whole_classes = ATT_UNROLL % nb == 0
            if whole_classes:
                starts = [(it * ATT_UNROLL + uu) // nb + dil * Q * (uu % nb) for uu in range(ATT_UNROLL)]
                k_own = [_bf(k_ref[0, _rows(s, Q, dil), :]) for s in starts]
                v_own = [_bf(v_ref[0, _rows(s, Q, dil), :]) for s in starts]
            for uu in range(ATT_UNROLL):
                if whole_classes:
                    n = uu % nb
                    start = starts[uu]
                    lo = uu - 1 if n > 0 else uu
                    hi = uu + 1 if n < nb - 1 else uu
                    kw = jnp.concatenate([k_own[lo], k_own[uu], k_own[hi]], axis=0)
                    vw = jnp.concatenate([v_own[lo], v_own[uu], v_own[hi]], axis=0)
                else:
                    i = it * ATT_UNROLL + uu
                    n = i % nb
                    start = pl.multiple_of(i * Q, Q) if dil == 1 else i // nb + dil * Q * n
                    s_prev = start - jnp.where(n > 0, dil * Q, 0)
                    s_next = start + jnp.where(n < nb - 1, dil * Q, 0)
                    if dil == 1:
                        s_prev, s_next = pl.multiple_of(s_prev, Q), pl.multiple_of(s_next, Q)
                    kw = _bf(jnp.concatenate([k_ref[0, _rows(s, Q, dil), :] for s in (s_prev, start, s_next)], axis=0))
                    vw = _bf(jnp.concatenate([v_ref[0, _rows(s, Q, dil), :] for s in (s_prev, start, s_next)], axis=0))
                q = q_ref[0, _rows(start, Q, dil), :] * (HEAD_DIM ** -0.5)
                pen_prev = jnp.where(n == 0, NEG_INF, 0.0)
                pen_next = jnp.where(n == nb - 1, NEG_INF, 0.0)
                edge = jnp.where(kj_row < Q, pen_prev, jnp.where(kj_row >= 2 * Q, pen_next, 0.0))
                blocks.append((start, q, kw, jnp.concatenate([vw, ones_v], axis=1), edge))
            scores = [_dot_nt(jnp.concatenate([jnp.where(left, q, 0.0), jnp.where(left, 0.0, q)], axis=0), kw)
                      + bias + edge for (_, q, kw, _, edge) in blocks]
            mx = [jnp.max(s, axis=-1, keepdims=True) for s in scores]
            pr = [jnp.exp(s - m) for s, m in zip(scores, mx)]
            pv = [jnp.dot(_bf(p), blk[3], preferred_element_type=F32) for p, blk in zip(pr, blocks)]
            for uu, blk in enumerate(blocks):
                rows = _rows(blk[0], Q, dil)
                den = jnp.where(left, pv[uu][0:Q, LANES:2 * LANES], pv[uu][Q:2 * Q, LANES:2 * LANES])
                o_s[bi, rows, :] = jnp.where(left, pv[uu][0:Q, 0:LANES], pv[uu][Q:2 * Q, 0:LANES]) / den
                lse_s[bi, rows, :] = jnp.where(left, mx[uu][0:Q], mx[uu][Q:2 * Q]) + jnp.log(den)
            return carry

        lax.fori_loop(0, S // Q // ATT_UNROLL, body, 0)

    def merge(j, carry):
        rows = pl.ds(pl.multiple_of(j * Q, Q), Q)
        ms = [lse_s[b, rows, :] for b in range(3)]
        top = jnp.maximum(jnp.maximum(ms[0], ms[1]), ms[2])
        ws = [jnp.exp(m - top) for m in ms]
        num = ws[0] * o_s[0, rows, :] + ws[1] * o_s[1, rows, :] + ws[2] * o_s[2, rows, :]
        out_ref[0, rows, :] = num / (ws[0] + ws[1] + ws[2])
        return carry

    lax.fori_loop(0, S // Q, merge, 0)


def _dilated_attention(q, k, v):
    B, S, W = q.shape
    pairs = W // LANES
    slopes = jnp.exp2(-8.0 * jnp.arange(1, ATT_HEADS + 1, dtype=F32) / ATT_HEADS).reshape(pairs, 2, 1)
    slopes = jnp.broadcast_to(jnp.pad(slopes, ((0, 0), (0, 6), (0, 0))), (pairs, 8, LANES))
    spec = pl.BlockSpec((1, S, LANES), lambda b, p: (b, 0, p))
    return pl.pallas_call(
        functools.partial(_att_kernel, seq=S),
        grid=(B, pairs),
        in_specs=[pl.BlockSpec((1, 8, LANES), lambda b, p: (p, 0, 0)), spec, spec, spec],
        out_specs=spec,
        out_shape=jax.ShapeDtypeStruct((B, S, W), F32),
        scratch_shapes=[pltpu.VMEM((3, S, LANES), F32)] * 2,
        compiler_params=pltpu.CompilerParams(dimension_semantics=("arbitrary", "arbitrary"),
                                             vmem_limit_bytes=VMEM_LIMIT),
        name="dilated_attention",
    )(slopes, q, k, v)


N_EXPERTS = 32
TOP_K = 4
SWIGLU_LIMIT = 7.0
SWIGLU_ALPHA = 1.702
MOE_ROWS = 512


CAST_ROWS = 128


def _expert_kernel(be_ref, nb_ref, x_ref, wgu_ref, bgu_ref, wd_ref, bd_ref, o_ref, wgu_s, wd_s):
    i = pl.program_id(0)
    F = wd_ref.shape[1]
    used = i < nb_ref[0]

    @pl.when(used & ((i == 0) | (be_ref[i] != be_ref[jnp.maximum(i - 1, 0)])))
    def _():
        def cast(src, dst):
            def step(j, carry):
                rows = pl.ds(pl.multiple_of(j * CAST_ROWS, CAST_ROWS), CAST_ROWS)
                dst[rows, :] = _bf(src[0, rows, :])
                return carry
            lax.fori_loop(0, src.shape[1] // CAST_ROWS, step, 0)
        cast(wgu_ref, wgu_s)
        cast(wd_ref, wd_s)

    @pl.when(used)
    def _():
        gu = jnp.dot(x_ref[...], wgu_s[...], preferred_element_type=F32) + bgu_ref[0]
        gate = jnp.minimum(gu[:, :F], SWIGLU_LIMIT)
        up = jnp.clip(gu[:, F:], -SWIGLU_LIMIT, SWIGLU_LIMIT)
        act = (up + 1.0) * gate * _sigmoid(SWIGLU_ALPHA * gate)
        o_ref[...] = (jnp.dot(_bf(act), wd_s[...], preferred_element_type=F32) + bd_ref[0]).astype(o_ref.dtype)

    @pl.when(jnp.logical_not(used))
    def _():
        o_ref[...] = jnp.zeros_like(o_ref)


def _expert_blocks(block_e, n_used, x_rows, w_gate_up, b_gate_up, w_down, b_down):
    n_rows, D = x_rows.shape
    E, _, F2 = w_gate_up.shape
    F = F2 // 2
    n_blocks = n_rows // MOE_ROWS
    return pl.pallas_call(
        _expert_kernel,
        grid_spec=pltpu.PrefetchScalarGridSpec(
            num_scalar_prefetch=2,
            grid=(n_blocks,),
            in_specs=[pl.BlockSpec((MOE_ROWS, D), lambda i, be, nb: (i, 0)),
                      pl.BlockSpec((1, D, F2), lambda i, be, nb: (be[i], 0, 0)),
                      pl.BlockSpec((1, 1, F2), lambda i, be, nb: (be[i], 0, 0)),
                      pl.BlockSpec((1, F, D), lambda i, be, nb: (be[i], 0, 0)),
                      pl.BlockSpec((1, 1, D), lambda i, be, nb: (be[i], 0, 0))],
            out_specs=pl.BlockSpec((MOE_ROWS, D), lambda i, be, nb: (i, 0)),
            scratch_shapes=[pltpu.VMEM((D, F2), BF16), pltpu.VMEM((F, D), BF16)]),
        out_shape=jax.ShapeDtypeStruct((n_rows, D), BF16),
        compiler_params=pltpu.CompilerParams(dimension_semantics=("arbitrary",),
                                             vmem_limit_bytes=VMEM_LIMIT),
        name="moe_experts",
    )(block_e, n_used, x_rows, w_gate_up, b_gate_up.reshape(E, 1, F2), w_down, b_down.reshape(E, 1, D))


def _route(top_e, n_tokens):
    T = n_tokens
    n_rows = T * TOP_K + N_EXPERTS * MOE_ROWS
    n_blocks = n_rows // MOE_ROWS
    experts = jnp.arange(N_EXPERTS, dtype=jnp.int32)
    onehot = jnp.sum((top_e[:, :, None] == experts).astype(jnp.int32), axis=1)
    incl = jnp.cumsum(onehot, axis=0)
    counts = incl[-1]
    padded = (counts + MOE_ROWS - 1) // MOE_ROWS * MOE_ROWS
    starts = jnp.cumsum(counts) - counts
    pends = jnp.cumsum(padded)
    pstarts = pends - padded
    rank = jnp.take_along_axis(incl - onehot, top_e, axis=1)
    dest = pstarts[top_e] + rank
    order = jnp.argsort(top_e.reshape(-1), stable=True).astype(jnp.int32)
    block_start = jnp.arange(n_blocks, dtype=jnp.int32) * MOE_ROWS
    block_e = jnp.minimum(jnp.sum((pends[None, :] <= block_start[:, None]).astype(jnp.int32), axis=1),
                          N_EXPERTS - 1)
    row = jnp.arange(n_rows, dtype=jnp.int32)
    within = row - jnp.repeat(pstarts[block_e], MOE_ROWS)
    src = jnp.clip(within + jnp.repeat(starts[block_e], MOE_ROWS), 0, T * TOP_K - 1)
    row_token = jnp.where(within < jnp.repeat(counts[block_e], MOE_ROWS),
                          jnp.take(order, src, mode='clip') // TOP_K, row % T)
    n_used = (pends[-1:] // MOE_ROWS).astype(jnp.int32)
    return dest.astype(jnp.int32), row_token.astype(jnp.int32), block_e.astype(jnp.int32), n_used


LN_EPS = 1e-5
MEM_HEADS = 4
MEM_WIDTH = MEM_HEADS * HEAD_DIM
RWKV_IN = 3 * RWKV_WIDTH + GATE_LORA + 2 * DECAY_LORA + 2 * ICLR_LORA
PROJ_ROWS = 512
BATCH_GROUPS = 1


def _ln(x, g, b):
    mu = jnp.mean(x, -1, keepdims=True)
    xc = x - mu
    var = jnp.mean(xc * xc, -1, keepdims=True)
    return xc * lax.rsqrt(var + LN_EPS) * g + b


HALO = 8


def _in_proj_kernel(x_ref, xp_ref, xn_ref, g_ref, b_ref, w_ref, mu_ref, mem_ref, wkv_ref,
                    rw_ref, q_ref, k_ref, v_ref, qm_ref, kv_s, *, seq):
    R = PROJ_ROWS
    i = pl.program_id(0)
    xs = jnp.concatenate([x_ref[...], xp_ref[...], xn_ref[...]], axis=0)
    hb = _bf(_ln(xs, g_ref[...], b_ref[...]))
    n_rw = rw_ref.shape[-1]
    p = jnp.dot(hb, w_ref[:, 0:n_rw], preferred_element_type=F32)
    cur = p[0:R]
    tiles_per_seq = seq // R
    first = (i % tiles_per_seq) == 0
    last = (i % tiles_per_seq) == tiles_per_seq - 1
    prev_row = jnp.where(first, 0.0, p[R + HALO - 1:R + HALO])
    next_row = jnp.where(last, 0.0, p[R + HALO:R + HALO + 1])
    rowi = lax.broadcasted_iota(jnp.int32, (R, 1), 0)
    prev = jnp.where(rowi == 0, prev_row, pltpu.roll(cur, 1, 0))
    nxt = jnp.where(rowi == R - 1, next_row, pltpu.roll(cur, R - 1, 0))
    rw_ref[...] = cur + mu_ref[...] * (0.5 * (prev + nxt) - cur)
    o = n_rw
    hb_main = hb[0:R]
    for ref in (q_ref, k_ref, v_ref):
        n = ref.shape[-1]
        ref[...] = jnp.dot(hb_main, w_ref[:, o:o + n], preferred_element_type=F32)
        o += n

    @pl.when(first)
    def _():
        kv_s[...] = _bf(jnp.dot(_bf(mem_ref[0]), wkv_ref[...], preferred_element_type=F32))

    qm = jnp.dot(hb_main, w_ref[:, o:o + MEM_WIDTH], preferred_element_type=F32)
    qm_ref[...] = _mem_heads(qm * (HEAD_DIM ** -0.5), kv_s)


def _in_proj(xt, ln_g, ln_b, w_pad, mu_pad, seq, mem, w_kv):
    T, D = xt.shape
    _, M, _ = mem.shape
    tiles_per_seq = seq // PROJ_ROWS
    widths = (RWKV_PAD_IN, ATT_WIDTH, ATT_WIDTH, ATT_WIDTH, MEM_WIDTH)
    assert sum(widths) == w_pad.shape[1] and seq % PROJ_ROWS == 0
    per_tile = PROJ_ROWS // HALO
    last_halo = T // HALO - 1
    rows = lambda n: pl.BlockSpec((PROJ_ROWS, n), lambda i: (i, 0))
    full = lambda a: pl.BlockSpec(a.shape, lambda i: (0,) * a.ndim)
    consts = (ln_g.reshape(1, D), ln_b.reshape(1, D), w_pad, mu_pad.reshape(1, RWKV_PAD_IN))
    return pl.pallas_call(
        functools.partial(_in_proj_kernel, seq=seq),
        grid=(T // PROJ_ROWS,),
        in_specs=[rows(D),
                  pl.BlockSpec((HALO, D), lambda i: (jnp.maximum(i * per_tile - 1, 0), 0)),
                  pl.BlockSpec((HALO, D), lambda i: (jnp.minimum((i + 1) * per_tile, last_halo), 0))]
                 + [full(a) for a in consts]
                 + [pl.BlockSpec((1, M, D), lambda i: (i // tiles_per_seq, 0, 0)),
                    pl.BlockSpec((D, 2 * MEM_WIDTH), lambda i: (0, 0))],
        out_specs=[rows(n) for n in widths],
        out_shape=[jax.ShapeDtypeStruct((T, n), F32) for n in widths],
        scratch_shapes=[pltpu.VMEM((M, 2 * MEM_WIDTH), BF16)],
        compiler_params=pltpu.CompilerParams(dimension_semantics=("arbitrary",), vmem_limit_bytes=VMEM_LIMIT),
        name="in_proj",
    )(xt, xt, xt, *consts, mem, _bf(w_kv))


def _mem_heads(q, kv_s):
    W = MEM_WIDTH
    lane = lax.broadcasted_iota(jnp.int32, (1, W), 1)
    kmat = kv_s[:, 0:W]
    vmat = kv_s[:, W:2 * W]
    out = jnp.zeros(q.shape, F32)
    for h in range(MEM_HEADS):
        sel = (lane >= h * HEAD_DIM) & (lane < (h + 1) * HEAD_DIM)
        s = lax.dot_general(_bf(jnp.where(sel, q, 0.0)), kmat, (((1,), (1,)), ((), ())),
                            preferred_element_type=F32)
        p = jnp.exp(s - jnp.max(s, axis=-1, keepdims=True))
        pv = jnp.dot(_bf(p), vmat, preferred_element_type=F32)
        out = jnp.where(sel, pv / jnp.sum(p, axis=-1, keepdims=True), out)
    return out


def _out_proj_kernel(x_ref, yr_ref, ya_ref, ym_ref, g0_ref, b0_ref, w_ref, g1_ref, b1_ref, wrh_ref, wrl_ref,
                     br_ref, h1_ref, h1b_ref, gate_ref, expert_ref, *, alpha):
    h0 = _ln(x_ref[...], g0_ref[...], b0_ref[...])
    o = 0
    mix = None
    for ref in (yr_ref, ya_ref, ym_ref):
        n = ref.shape[-1]
        part = jnp.dot(_bf(ref[...]), w_ref[o:o + n, :], preferred_element_type=F32)
        mix = part if mix is None else mix + part
        o += n
    h1 = _ln(alpha * h0 + mix, g1_ref[...], b1_ref[...])
    h1_ref[...] = h1
    hi = _bf(h1)
    h1b_ref[...] = hi
    lo = _bf(h1 - hi.astype(F32))
    logits = (jnp.dot(hi, wrh_ref[...], preferred_element_type=F32)
              + jnp.dot(lo, wrh_ref[...], preferred_element_type=F32)
              + jnp.dot(hi, wrl_ref[...], preferred_element_type=F32)) + br_ref[...]
    lane = lax.broadcasted_iota(jnp.int32, logits.shape, 1).astype(F32)
    vals = logits
    top_v, top_i = [], []
    for _ in range(TOP_K):
        m = jnp.max(vals, axis=-1, keepdims=True)
        idx = jnp.min(jnp.where(vals == m, lane, float(LANES)), axis=-1, keepdims=True)
        top_v.append(m)
        top_i.append(idx)
        vals = jnp.where(lane == idx, -jnp.inf, vals)
    ex = [jnp.exp(v - top_v[0]) for v in top_v]
    den = ex[0] + ex[1] + ex[2] + ex[3]
    gates = jnp.zeros(logits.shape, F32)
    experts = jnp.zeros(logits.shape, F32)
    for k in range(TOP_K):
        gates = jnp.where(lane == k, ex[k] / den, gates)
        experts = jnp.where(lane == k, top_i[k], experts)
    gate_ref[...] = gates
    expert_ref[...] = experts.astype(jnp.int32)


def _out_proj(xt, y_rwkv, y_att, y_mem, ln0_g, ln0_b, w_out, ln1_g, ln1_b, w_router, b_router, alpha):
    T, D = xt.shape
    E = w_router.shape[1]
    wr = jnp.pad(w_router, ((0, 0), (0, LANES - E)))
    wr_hi = _bf(wr)
    wr_lo = _bf(wr - wr_hi.astype(F32))
    br = jnp.pad(b_router, (0, LANES - E), constant_values=NEG_INF).reshape(1, LANES)
    rows = lambda n: pl.BlockSpec((PROJ_ROWS, n), lambda i: (i, 0))
    full = lambda a: pl.BlockSpec(a.shape, lambda i: (0,) * a.ndim)
    vec = lambda t: t.reshape(1, D)
    consts = (vec(ln0_g), vec(ln0_b), _bf(w_out), vec(ln1_g), vec(ln1_b), wr_hi, wr_lo, br)
    return pl.pallas_call(
        functools.partial(_out_proj_kernel, alpha=alpha),
        grid=(T // PROJ_ROWS,),
        in_specs=[rows(D), rows(y_rwkv.shape[1]), rows(y_att.shape[1]), rows(y_mem.shape[1])]
                 + [full(a) for a in consts],
        out_specs=[rows(D), rows(D), rows(LANES), rows(LANES)],
        out_shape=[jax.ShapeDtypeStruct((T, D), F32), jax.ShapeDtypeStruct((T, D), BF16),
                   jax.ShapeDtypeStruct((T, LANES), F32), jax.ShapeDtypeStruct((T, LANES), jnp.int32)],
        compiler_params=pltpu.CompilerParams(dimension_semantics=("arbitrary",), vmem_limit_bytes=VMEM_LIMIT),
        name="out_proj_router",
    )(xt, y_rwkv, y_att, y_mem, *consts)


def _combine_kernel(h1_ref, y_ref, gate_ref, g_ref, b_ref, o_ref, *, alpha):
    acc = alpha * h1_ref[...]
    gates = gate_ref[...]
    for k in range(TOP_K):
        acc = acc + gates[:, k:k + 1] * y_ref[k].astype(F32)
    o_ref[...] = _ln(acc, g_ref[...], b_ref[...])


def _combine(h1, y_top, gates, ln_g, ln_b, alpha):
    T, D = h1.shape
    rows = lambda n: pl.BlockSpec((PROJ_ROWS, n), lambda i: (i, 0))
    full = lambda a: pl.BlockSpec(a.shape, lambda i: (0,) * a.ndim)
    consts = (ln_g.reshape(1, D), ln_b.reshape(1, D))
    return pl.pallas_call(
        functools.partial(_combine_kernel, alpha=alpha),
        grid=(T // PROJ_ROWS,),
        in_specs=[rows(D), pl.BlockSpec((TOP_K, PROJ_ROWS, D), lambda i: (0, i, 0)), rows(LANES)]
                 + [full(a) for a in consts],
        out_specs=rows(D),
        out_shape=jax.ShapeDtypeStruct((T, D), F32),
        compiler_params=pltpu.CompilerParams(dimension_semantics=("arbitrary",), vmem_limit_bytes=VMEM_LIMIT),
        name="combine_ln",
    )(h1, y_top, gates, *consts)


def kernel(x, mem, ln_in_g, ln_in_b, w_in, mu_shift, w0, w_up, a0, a_up, g_up, k_k, k_a, r_k, gn_g, gn_b, w_mem_kv, w_out, ln1_g, ln1_b, w_router, b_router, w_gate_up, b_gate_up, w_down, b_down, ln2_g, ln2_b):
    B, S, D = x.shape
    depth = w_in.shape[0]
    assert depth == 1, "the layer norm feeding a layer is fused into its projection kernels"
    assert B % BATCH_GROUPS == 0
    alpha = (2 * depth) ** 0.25
    l = 0
    pad = RWKV_PAD_IN - RWKV_IN
    w_pad = _bf(jnp.concatenate([w_in[l][:, :RWKV_IN], jnp.zeros((D, pad), F32), w_in[l][:, RWKV_IN:]], axis=1))
    mu_pad = jnp.pad(mu_shift[l], (0, pad))

    def layer(xg, memg):
        Bg = xg.shape[0]
        T = Bg * S
        xt = xg.reshape(T, D)
        p_rwkv, q_a, k_a_, v_a, y_mem = _in_proj(xt, ln_in_g, ln_in_b, w_pad, mu_pad, S, memg, w_mem_kv[l])
        seq = lambda t: t.reshape(Bg, S, t.shape[-1])
        y_rwkv = _rwkv_mixer(seq(p_rwkv), w0[l], w_up[l], a0[l], a_up[l], g_up[l], k_k[l], k_a[l], r_k[l],
                             gn_g[l], gn_b[l])
        y_att = _dilated_attention(seq(q_a), seq(k_a_), seq(v_a))
        flat = lambda t: t.reshape(T, t.shape[-1])
        h1, h1b, gates, experts = _out_proj(xt, flat(y_rwkv), flat(y_att), flat(y_mem), ln_in_g, ln_in_b, w_out[l],
                                            ln1_g[l], ln1_b[l], w_router[l], b_router[l], alpha)
        dest, row_token, block_e, n_used = _route(experts[:, :TOP_K], T)
        x_rows = jnp.take(h1b, row_token, axis=0, mode='clip')
        y_rows = _expert_blocks(block_e, n_used, x_rows, w_gate_up[l], b_gate_up[l], w_down[l], b_down[l])
        y_top = jnp.take(y_rows, dest.T.reshape(-1), axis=0, mode='clip').reshape(TOP_K, T, D)
        return _combine(h1, y_top, gates, ln2_g[l], ln2_b[l], alpha).reshape(Bg, S, D)

    Bg = B // BATCH_GROUPS
    outs = [layer(x[g * Bg:(g + 1) * Bg], mem[g * Bg:(g + 1) * Bg]) for g in range(BATCH_GROUPS)]
    return outs[0] if BATCH_GROUPS == 1 else jnp.concatenate(outs, axis=0)
```
